```python
import jax
import jax.numpy as jnp
from jax import lax
import numpy as np

D_MODEL = 1024
BATCH = 2
SEQ = 8192
DEPTH = 2
DEC_BATCH = 8
DEC_SEQ = 64
PAST_LEN = 4096

CHUNK = 64
Q_BLOCK = 128
PLE_DIM = 256
N_HEADS = 4
BRANCH_WIDTH = D_MODEL // 2
HEAD_DIM = BRANCH_WIDTH // N_HEADS
CONV_WIDTH = 4
N_BRANCH = 3
RMS_EPS = 1e-6
IN_SIZES = (BRANCH_WIDTH,) * 5 + (N_HEADS, N_HEADS) + (BRANCH_WIDTH,) * 8 + (D_MODEL,) * N_BRANCH
N_IN = sum(IN_SIZES)
SPLIT_POINTS = tuple(int(s) for s in np.cumsum(IN_SIZES)[:-1])

kernel_name = "hybrid_mlstm_stickbreak_hgrn2_stream_step"


def rms_norm(x, g):
    xf = x.astype(jnp.float32)
    return xf * lax.rsqrt(jnp.mean(xf * xf, -1, keepdims=True) + RMS_EPS) * g.astype(jnp.float32)


def split_heads(a):
    B, T, _ = a.shape
    return a.reshape(B, T, N_HEADS, -1).transpose(0, 2, 1, 3)


def merge_heads(a):
    B, H, T, d = a.shape
    return a.transpose(0, 2, 1, 3).reshape(B, T, H * d)


def head_rms_norm(h, g):
    hn = h * lax.rsqrt(jnp.mean(h * h, -1, keepdims=True) + RMS_EPS)
    return merge_heads(hn) * g.astype(jnp.float32)


def to_chunks(a, chunk):
    B, H, T = a.shape[:3]
    return jnp.moveaxis(a.reshape(B, H, T // chunk, chunk, *a.shape[3:]), 2, 0)


def from_chunks(a):
    a = jnp.moveaxis(a, 0, 2)
    return a.reshape(a.shape[0], a.shape[1], -1, *a.shape[4:])


def causal_depthwise_conv(xin, buf, w):
    T = xin.shape[1]
    xp = jnp.concatenate([buf.astype(jnp.float32), xin], axis=1)
    wf = w.astype(jnp.float32)
    out = sum(xp[:, j:j + T] * wf[j] for j in range(CONV_WIDTH))
    return out, xp[:, T:]


def mlstm_chunkwise(q, k, v, ig, lf, C0, n0, m0, chunk):
    causal = jnp.tril(jnp.ones((chunk, chunk), dtype=bool))

    def step(carry, inp):
        C, n, m = carry
        qc, kc, vc, igc, lfc = inp
        b = jnp.cumsum(lfc, axis=-1)
        dmat = jnp.where(causal, b[..., :, None] - b[..., None, :] + igc[..., None, :], -jnp.inf)
        inter = b + m[..., None]
        m_t = jnp.maximum(jnp.max(dmat, -1), inter)
        s = jnp.einsum('bhtd,bhsd->bhts', qc, kc) * jnp.exp(dmat - m_t[..., None])
        w_inter = jnp.exp(inter - m_t)
        num = jnp.einsum('bhts,bhse->bhte', s, vc) + w_inter[..., None] * jnp.einsum('bhtd,bhde->bhte', qc, C)
        den = jnp.sum(s, -1) + w_inter * jnp.einsum('bhtd,bhd->bht', qc, n)
        h = num / jnp.maximum(jnp.abs(den), jnp.exp(-m_t))[..., None]
        m_new = m_t[..., -1]
        dec_s = jnp.exp(b[..., -1:] - b + igc - m_new[..., None])
        dec_c = jnp.exp(b[..., -1] + m - m_new)
        C_new = dec_c[..., None, None] * C + jnp.einsum('bhs,bhsd,bhse->bhde', dec_s, kc, vc)
        n_new = dec_c[..., None] * n + jnp.einsum('bhs,bhsd->bhd', dec_s, kc)
        return (C_new, n_new, m_new), h

    xs = (to_chunks(q, chunk), to_chunks(k, chunk), to_chunks(v, chunk), to_chunks(ig, chunk), to_chunks(lf, chunk))
    (C, n, m), h = lax.scan(step, (C0, n0, m0), xs)
    return from_chunks(h), C, n, m


def hgrn2_chunkwise(q, k, g, i, S0, chunk):
    causal = jnp.tril(jnp.ones((chunk, chunk), dtype=bool))

    def step(S, inp):
        qc, kc, gc, ic = inp
        bc = jnp.cumsum(gc, axis=2)
        diff = jnp.where(causal[:, :, None], bc[:, :, :, None, :] - bc[:, :, None, :, :], -jnp.inf)
        a = jnp.einsum('bhtc,bhsc,bhtsc->bhts', qc, kc, jnp.exp(diff))
        o = jnp.einsum('bhts,bhsv->bhtv', a, ic) + jnp.einsum('bhtc,bhcv->bhtv', qc * jnp.exp(bc), S)
        S_new = jnp.exp(bc[:, :, -1])[..., None] * S + jnp.einsum('bhsc,bhsv->bhcv', kc * jnp.exp(bc[:, :, -1:] - bc), ic)
        return S_new, o

    xs = (to_chunks(q, chunk), to_chunks(k, chunk), to_chunks(g, chunk), to_chunks(i, chunk))
    S, o = lax.scan(step, S0, xs)
    return from_chunks(o), S


def sb_attend(q, k, v, q_pos):
    z = jnp.einsum('bhqd,bhkd->bhqk', q, k) * (HEAD_DIM ** -0.5)
    before = jnp.arange(k.shape[2])[None, :] < q_pos[:, None]
    log_1mb = jnp.where(before, -jax.nn.softplus(z), 0.0)
    suffix = lax.cumsum(log_1mb, axis=3, reverse=True) - log_1mb
    a = jnp.where(before, jnp.exp(jax.nn.log_sigmoid(z) + suffix), 0.0)
    return jnp.einsum('bhqk,bhkd->bhqd', a, v)


def sb_prompt(q, k, v):
    B, H, T, d = q.shape
    nb = T // Q_BLOCK
    qb = jnp.moveaxis(q.reshape(B, H, nb, Q_BLOCK, d), 2, 0)
    pos = jnp.arange(T).reshape(nb, Q_BLOCK)
    out = lax.map(lambda a: sb_attend(a[0], k, v, a[1]), (qb, pos))
    return jnp.moveaxis(out, 0, 2).reshape(B, H, T, d)


def mixer_layer(x, p, C0, n0, m0, conv0, S0, k_past, v_past, lb,
                g_pre, w_in, b_i, b_f, w_conv, g_ml, g_hg, w_branch, w_out, g_post, w_pg, w_pp):
    f32 = jnp.float32
    T = x.shape[1]
    chunk = min(CHUNK, T)
    u = rms_norm(x, g_pre)
    proj = (u @ w_in).astype(f32)
    (ml_q, ml_k, ml_v, ml_o, ml_z, ml_i, ml_f, sb_q, sb_k, sb_v, sb_z,
     hg_q, hg_f, hg_i, hg_z, gt_ml, gt_sb, gt_hg) = jnp.split(proj, SPLIT_POINTS, axis=-1)

    qk, conv_new = causal_depthwise_conv(jnp.concatenate([ml_q, ml_k], -1), conv0, w_conv)
    q_m, k_m = jnp.split(jax.nn.silu(qk), 2, axis=-1)
    ig = (ml_i + b_i.astype(f32)).transpose(0, 2, 1)
    lf = jax.nn.log_sigmoid(ml_f + b_f.astype(f32)).transpose(0, 2, 1)
    hm, C_new, n_new, m_new = mlstm_chunkwise(split_heads(q_m), split_heads(k_m) * (HEAD_DIM ** -0.5), split_heads(ml_v),
                                              ig, lf, C0.astype(f32), n0.astype(f32), m0.astype(f32), chunk)
    h_ml = head_rms_norm(hm, g_ml) * jax.nn.sigmoid(ml_o) * jax.nn.silu(ml_z)

    q_s, k_s, v_s = split_heads(sb_q), split_heads(sb_k), split_heads(sb_v)
    if k_past is None:
        o_s = sb_prompt(q_s, k_s, v_s)
    else:
        P = k_past.shape[2]
        k_all = jnp.concatenate([k_past.astype(f32), k_s], axis=2)
        v_all = jnp.concatenate([v_past.astype(f32), v_s], axis=2)
        o_s = sb_attend(q_s, k_all, v_all, P + jnp.arange(T))
    h_sb = merge_heads(o_s) * jax.nn.silu(sb_z)

    log_f = jnp.logaddexp(jnp.log(lb), jnp.log1p(-lb) + jax.nn.log_sigmoid(hg_f))
    k_h = (1.0 - lb) * jax.nn.sigmoid(-hg_f)
    o_h, S_new = hgrn2_chunkwise(split_heads(hg_q), split_heads(k_h), split_heads(log_f), split_heads(hg_i),
                                 S0.astype(f32), chunk)
    h_hg = head_rms_norm(o_h, g_hg) * jax.nn.silu(hg_z)

    merged = (jax.nn.sigmoid(gt_ml) * (h_ml @ w_branch[0])
              + jax.nn.sigmoid(gt_sb) * (h_sb @ w_branch[1])
              + jax.nn.sigmoid(gt_hg) * (h_hg @ w_branch[2]))
    y = merged @ w_out
    r = x.astype(f32) + rms_norm(y, g_post)
    r = r + jax.nn.sigmoid(r @ w_pg) * (p.astype(f32) @ w_pp)
    return r.astype(x.dtype), (C_new, n_new, m_new, conv_new, k_s, v_s, S_new)


def setup_inputs(seed: int = 0) -> dict:
    key = jax.random.key(seed)
    ks = jax.random.split(key, 32)
    n = jax.random.normal
    f32 = jnp.float32
    W, H, d = BRANCH_WIDTH, N_HEADS, HEAD_DIM
    return {
        "x_prompt": n(ks[0], (BATCH, SEQ, D_MODEL), f32),
        "x_sample": n(ks[1], (DEC_BATCH, DEC_SEQ, D_MODEL), f32),
        "p_prompt": n(ks[2], (DEPTH, BATCH, SEQ, PLE_DIM), f32),
        "p_sample": n(ks[3], (DEPTH, DEC_BATCH, DEC_SEQ, PLE_DIM), f32),
        "state_mlstm_C": 0.3 * n(ks[4], (DEPTH, DEC_BATCH, H, d, d), f32),
        "state_mlstm_n": 0.3 * n(ks[5], (DEPTH, DEC_BATCH, H, d), f32),
        "state_mlstm_m": 0.5 * n(ks[6], (DEPTH, DEC_BATCH, H), f32),
        "state_mlstm_conv": n(ks[7], (DEPTH, DEC_BATCH, CONV_WIDTH - 1, 2 * W), f32),
        "cache_sb_k": n(ks[8], (DEPTH, DEC_BATCH, H, PAST_LEN, d), f32),
        "cache_sb_v": n(ks[9], (DEPTH, DEC_BATCH, H, PAST_LEN, d), f32),
        "state_hgrn_S": n(ks[10], (DEPTH, DEC_BATCH, H, d, d), f32),
        "g_pre": 1.0 + 0.02 * n(ks[11], (DEPTH, D_MODEL), f32),
        "w_in": n(ks[12], (DEPTH, D_MODEL, N_IN), f32) * D_MODEL ** -0.5,
        "b_mlstm_i": 0.1 * n(ks[13], (DEPTH, H), f32),
        "b_mlstm_f": jnp.linspace(3.0, 6.0, H, dtype=f32)[None, :] + 0.1 * n(ks[14], (DEPTH, H), f32),
        "w_mlstm_conv": n(ks[15], (DEPTH, CONV_WIDTH, 2 * W), f32) * CONV_WIDTH ** -0.5,
        "g_mlstm_head": 1.0 + 0.02 * n(ks[16], (DEPTH, W), f32),
        "hgrn_lb_logits": 0.5 * n(ks[17], (DEPTH, W), f32),
        "g_hgrn_head": 1.0 + 0.02 * n(ks[18], (DEPTH, W), f32),
        "w_branch": n(ks[19], (DEPTH, N_BRANCH, W, D_MODEL), f32) * W ** -0.5,
        "w_out": n(ks[20], (DEPTH, D_MODEL, D_MODEL), f32) * D_MODEL ** -0.5,
        "g_post": 1.0 + 0.02 * n(ks[21], (DEPTH, D_MODEL), f32),
        "w_ple_gate": n(ks[22], (DEPTH, D_MODEL, D_MODEL), f32) * D_MODEL ** -0.5,
        "w_ple_proj": n(ks[23], (DEPTH, PLE_DIM, D_MODEL), f32) * PLE_DIM ** -0.5,
    }


def reference(x_prompt, x_sample, p_prompt, p_sample, state_mlstm_C, state_mlstm_n, state_mlstm_m,
              state_mlstm_conv, cache_sb_k, cache_sb_v, state_hgrn_S, g_pre, w_in, b_mlstm_i, b_mlstm_f,
              w_mlstm_conv, g_mlstm_head, hgrn_lb_logits, g_hgrn_head, w_branch, w_out, g_post,
              w_ple_gate, w_ple_proj):
    f32 = jnp.float32
    lb_cum = jnp.cumsum(jax.nn.softmax(hgrn_lb_logits.astype(f32), axis=0), axis=0)
    lower_bounds = lb_cum - lb_cum[:1]
    B = x_prompt.shape[0]
    yp, ys = x_prompt, x_sample
    new_p = [[] for _ in range(7)]
    new_s = [[] for _ in range(7)]
    for l in range(DEPTH):
        lw = (g_pre[l], w_in[l], b_mlstm_i[l], b_mlstm_f[l], w_mlstm_conv[l], g_mlstm_head[l], g_hgrn_head[l],
              w_branch[l], w_out[l], g_post[l], w_ple_gate[l], w_ple_proj[l])
        C0 = jnp.zeros((B, N_HEADS, HEAD_DIM, HEAD_DIM), f32)
        n0 = jnp.zeros((B, N_HEADS, HEAD_DIM), f32)
        m0 = jnp.zeros((B, N_HEADS), f32)
        conv0 = jnp.zeros((B, CONV_WIDTH - 1, 2 * BRANCH_WIDTH), f32)
        S0 = jnp.zeros((B, N_HEADS, HEAD_DIM, HEAD_DIM), f32)
        yp, st = mixer_layer(yp, p_prompt[l], C0, n0, m0, conv0, S0, None, None, lower_bounds[l], *lw)
        for lst, a in zip(new_p, st):
            lst.append(a)
        ys, st = mixer_layer(ys, p_sample[l], state_mlstm_C[l], state_mlstm_n[l], state_mlstm_m[l],
                             state_mlstm_conv[l], state_hgrn_S[l], cache_sb_k[l], cache_sb_v[l],
                             lower_bounds[l], *lw)
        for lst, a in zip(new_s, st):
            lst.append(a)
    pC, pn, pm, pconv, pk, pv, pS = [jnp.stack(a) for a in new_p]
    sC, sn, sm, sconv, sk, sv, sS = [jnp.stack(a) for a in new_s]
    return (yp, ys, pC, pn, pm, pconv, pk, pv, pS, sC, sn, sm, sconv, sk, sv, sS)
```

```python
import functools

import jax
import jax.numpy as jnp
from jax import lax
from jax.experimental import pallas as pl
from jax.experimental.pallas import tpu as pltpu

F32 = jnp.float32
BF16 = jnp.bfloat16

D_MODEL = 1024
PLE_DIM = 256
N_HEADS = 4
BRANCH_WIDTH = D_MODEL // 2
HEAD_DIM = BRANCH_WIDTH // N_HEADS
CONV_WIDTH = 4
RMS_EPS = 1e-6
CHUNK = 64
HALO = 8
NEG_BIG = -1e30

N_GATE_PAD = BRANCH_WIDTH
N_PROJ = N_GATE_PAD + 13 * BRANCH_WIDTH + 3 * D_MODEL
LANE = 128
C_IF, C_MLQ, C_MLK, C_MLV, C_MLO, C_MLZ = 0, 4, 8, 12, 16, 20
C_SBQ, C_SBK, C_SBV, C_SBZ = 24, 28, 32, 36
C_HGQ, C_HGF, C_HGI, C_HGZ = 40, 44, 48, 52
C_GATES = 56

VMEM_LIMIT = 56 * 1024 * 1024

NT_DIMS = (((1,), (1,)), ((), ()))
TN_DIMS = (((0,), (0,)), ((), ()))


def _sigmoid(x):
    return 1.0 / (1.0 + jnp.exp(-x))


def _log_sigmoid(x):
    return jnp.minimum(x, 0.0) - jnp.log1p(jnp.exp(-jnp.abs(x)))


def _dot(a, b, dims=None, precision=None):
    if dims is None:
        dims = (((a.ndim - 1,), (0,)), ((), ()))
    return lax.dot_general(a, b, dims, precision=precision, preferred_element_type=F32)


def _dot_exact(a, b, dims=None):
    return _dot(a, b, dims, precision=lax.Precision.HIGHEST)


def _inproj_kernel(x_ref, g_ref, w_ref, o_ref, u_ref):
    @pl.when(pl.program_id(1) == 0)
    def _():
        x = x_ref[...]
        ms = jnp.mean(x * x, axis=-1, keepdims=True)
        u_ref[...] = (x * lax.rsqrt(ms + RMS_EPS) * g_ref[...]).astype(BF16)

    o_ref[...] = jnp.dot(u_ref[...], w_ref[...], preferred_element_type=F32)


def _inproj(x2d, g_pre, w_bf16):
    n = x2d.shape[0]
    tm = min(n, 1024)
    tn = 1024
    return pl.pallas_call(
        _inproj_kernel,
        grid=(n // tm, N_PROJ // tn),
        in_specs=[
            pl.BlockSpec((tm, D_MODEL), lambda i, j: (i, 0)),
            pl.BlockSpec((1, D_MODEL), lambda i, j: (0, 0)),
            pl.BlockSpec((D_MODEL, tn), lambda i, j: (0, j)),
        ],
        out_specs=pl.BlockSpec((tm, tn), lambda i, j: (i, j)),
        out_shape=jax.ShapeDtypeStruct((n, N_PROJ), F32),
        scratch_shapes=[pltpu.VMEM((tm, D_MODEL), BF16)],
        compiler_params=pltpu.CompilerParams(
            dimension_semantics=("parallel", "arbitrary"), vmem_limit_bytes=VMEM_LIMIT),
        name="inproj",
    )(x2d, g_pre.reshape(1, D_MODEL), w_bf16)


def _mlstm_kernel(q_ref, k_ref, v_ref, gi_ref, conv0_ref, wc_ref, bias_ref, c0_ref, n0_ref, m0_ref,
                  hm_ref, c_ref, n_ref, m_ref, xs_ref, *, L):
    c = pl.program_id(1)

    @pl.when(c == 0)
    def _():
        c_ref[...] = c0_ref[...]
        n_ref[...] = n0_ref[...]
        m_ref[...] = m0_ref[...]
        xs_ref[0:HALO, :] = conv0_ref[...]

    xs_ref[HALO:HALO + L, 0:BRANCH_WIDTH] = q_ref[...]
    xs_ref[HALO:HALO + L, BRANCH_WIDTH:] = k_ref[...]
    base = HALO - (CONV_WIDTH - 1)
    qk = xs_ref[base:base + L, :] * wc_ref[0:1, :]
    for j in range(1, CONV_WIDTH):
        qk = qk + xs_ref[base + j:base + j + L, :] * wc_ref[j:j + 1, :]
    qk = qk * _sigmoid(qk)
    xs_ref[0:HALO, :] = xs_ref[L:L + HALO, :]

    gates = gi_ref[...] + bias_ref[...]
    row = lax.broadcasted_iota(jnp.int32, (L, L), 0)
    col = lax.broadcasted_iota(jnp.int32, (L, L), 1)
    causal = col <= row
    ltri = jnp.where(causal, 1.0, 0.0).astype(F32)
    bcum = _dot_exact(ltri, _log_sigmoid(gates))
    lane = lax.broadcasted_iota(jnp.int32, (L, LANE), 1)
    scale = HEAD_DIM ** -0.5

    for h in range(N_HEADS):
        ig = gates[:, h:h + 1]
        b = bcum[:, N_HEADS + h:N_HEADS + h + 1]
        lhs = jnp.where(lane == 0, b, jnp.where(lane == 1, 1.0, 0.0))
        rhs = jnp.where(lane == 0, 1.0, jnp.where(lane == 1, ig - b, 0.0))
        dmat = jnp.where(causal, _dot_exact(lhs, rhs, NT_DIMS), NEG_BIG)
        m_prev = m_ref[h][:, 0:1]
        inter = b + m_prev
        m_t = jnp.maximum(jnp.max(dmat, axis=-1, keepdims=True), inter)
        qh = qk[:, h * HEAD_DIM:(h + 1) * HEAD_DIM]
        kh = qk[:, BRANCH_WIDTH + h * HEAD_DIM:BRANCH_WIDTH + (h + 1) * HEAD_DIM] * scale
        vh = v_ref[:, h * HEAD_DIM:(h + 1) * HEAD_DIM]
        qb, kb, vb = qh.astype(BF16), kh.astype(BF16), vh.astype(BF16)
        s = _dot(qb, kb, NT_DIMS) * jnp.exp(dmat - m_t)
        w_inter = jnp.exp(inter - m_t)
        c_prev = c_ref[h]
        n_prev = n_ref[h]
        num = _dot(s.astype(BF16), vb) + w_inter * _dot(qb, c_prev.astype(BF16))
        den = jnp.sum(s, axis=-1, keepdims=True) + w_inter * jnp.sum(qh * n_prev, axis=-1, keepdims=True)
        hm_ref[:, h * HEAD_DIM:(h + 1) * HEAD_DIM] = num / jnp.maximum(jnp.abs(den), jnp.exp(-m_t))
        m_new = m_t[L - 1:L, :]
        b_last = b[L - 1:L, :]
        dec_s = jnp.exp(b_last - b + ig - m_new)
        dec_c = jnp.exp(b_last + m_prev - m_new)
        kd = kh * dec_s
        c_ref[h] = dec_c * c_prev + _dot(kd.astype(BF16), vb, TN_DIMS)
        n_ref[h] = dec_c * n_prev + jnp.sum(kd, axis=0, keepdims=True)
        m_ref[h] = jnp.broadcast_to(m_new, (1, LANE))


def _mlstm(proj, conv0_pad, w_conv, bias_row, c0, n0, m0, L):
    B, T, _ = proj.shape
    nc = T // L
    wide = lambda blk: pl.BlockSpec((None, L, BRANCH_WIDTH), lambda b, c: (b, c, blk))
    state = lambda shp: pl.BlockSpec((None,) + shp, lambda b, c: (b,) + (0,) * len(shp))
    return pl.pallas_call(
        functools.partial(_mlstm_kernel, L=L),
        grid=(B, nc),
        in_specs=[
            wide(C_MLQ // 4), wide(C_MLK // 4), wide(C_MLV // 4),
            pl.BlockSpec((None, L, LANE), lambda b, c: (b, c, C_IF)),
            state((HALO, D_MODEL)),
            pl.BlockSpec((CONV_WIDTH, D_MODEL), lambda b, c: (0, 0)),
            pl.BlockSpec((1, LANE), lambda b, c: (0, 0)),
            state((N_HEADS, HEAD_DIM, HEAD_DIM)), state((N_HEADS, 1, HEAD_DIM)), state((N_HEADS, 1, LANE)),
        ],
        out_specs=[
            pl.BlockSpec((None, L, BRANCH_WIDTH), lambda b, c: (b, c, 0)),
            state((N_HEADS, HEAD_DIM, HEAD_DIM)), state((N_HEADS, 1, HEAD_DIM)), state((N_HEADS, 1, LANE)),
        ],
        out_shape=[
            jax.ShapeDtypeStruct((B, T, BRANCH_WIDTH), F32),
            jax.ShapeDtypeStruct((B, N_HEADS, HEAD_DIM, HEAD_DIM), F32),
            jax.ShapeDtypeStruct((B, N_HEADS, 1, HEAD_DIM), F32),
            jax.ShapeDtypeStruct((B, N_HEADS, 1, LANE), F32),
        ],
        scratch_shapes=[pltpu.VMEM((L + HALO, D_MODEL), F32)],
        compiler_params=pltpu.CompilerParams(
            dimension_semantics=("parallel", "arbitrary"), vmem_limit_bytes=VMEM_LIMIT),
        name="mlstm",
    )(proj, proj, proj, proj, conv0_pad, w_conv, bias_row, c0, n0, m0)


HGRN_LEVELS = (32, 16, 8)
HGRN_DIAG = 8


def _hgrn_kernel(q_ref, f_ref, i_ref, la_ref, l1_ref, oml_ref, s0_ref, o_ref, st_ref, *, L):
    c = pl.program_id(1)

    @pl.when(c == 0)
    def _():
        st_ref[...] = s0_ref[...]

    f = f_ref[...]
    la = la_ref[...]
    lb_term = l1_ref[...] + _log_sigmoid(f)
    g = jnp.maximum(la, lb_term) + jnp.log1p(jnp.exp(-jnp.abs(la - lb_term)))
    kk = oml_ref[...] / (1.0 + jnp.exp(f))

    row = lax.broadcasted_iota(jnp.int32, (L, L), 0)
    col = lax.broadcasted_iota(jnp.int32, (L, L), 1)
    ones01 = lambda m: jnp.where(m, 1.0, 0.0).astype(F32)
    bc = _dot_exact(ones01(col <= row), g)
    bc_last = bc[L - 1:L, :]
    levels = [lv for lv in HGRN_LEVELS if lv < L]
    eq, ek = {}, {}
    for lv in levels:
        same = (row // lv) == (col // lv)
        eq[lv] = _dot_exact(ones01(same & (col <= row)), g)
        ek[lv] = _dot_exact(ones01(same & (col > row)), g)
    wb = eq[HGRN_DIAG]
    lane = lax.broadcasted_iota(jnp.int32, (L, LANE), 1)
    rowl = lax.broadcasted_iota(jnp.int32, (L, LANE), 0)
    sub3 = lax.broadcasted_iota(jnp.int32, (L // HGRN_DIAG, HGRN_DIAG, LANE), 1)
    ones_bf = jnp.ones((LANE, LANE), BF16)
    q_all = q_ref[...]

    for h in range(N_HEADS):
        sl = slice(h * HEAD_DIM, (h + 1) * HEAD_DIM)
        qh, kh, ih = q_all[:, sl], kk[:, sl], i_ref[:, sl]
        ib = ih.astype(BF16)
        st = st_ref[h]
        out = _dot((qh * jnp.exp(bc[:, sl])).astype(BF16), st.astype(BF16), NT_DIMS)
        at = jnp.zeros((L, L), F32)
        for lv in levels:
            qt = (qh * jnp.exp(eq[lv][:, sl])).astype(BF16)
            kt = (kh * jnp.exp(ek[lv][:, sl])).astype(BF16)
            pair = ((col // lv) % 2 == 1) & ((row // lv) == (col // lv) - 1)
            at = at + jnp.where(pair, _dot(kt, qt, NT_DIMS), 0.0)
        nb = L // HGRN_DIAG
        q3 = qh.reshape(nb, HGRN_DIAG, HEAD_DIM)
        k3 = kh.reshape(nb, HGRN_DIAG, HEAD_DIM)
        w3 = wb[:, sl].reshape(nb, HGRN_DIAG, HEAD_DIM)
        for tl in range(HGRN_DIAG):
            dec = jnp.exp(jnp.minimum(w3[:, tl:tl + 1, :] - w3, 0.0))
            p = jnp.where(sub3 <= tl, q3[:, tl:tl + 1, :] * k3 * dec, 0.0).reshape(L, HEAD_DIM)
            r = _dot(p.astype(BF16), ones_bf)
            sel = lane == (rowl // HGRN_DIAG) * HGRN_DIAG + tl
            at = at + jnp.where(sel, r, 0.0)[:, 0:L]
        out = out + _dot(at.astype(BF16), ib, TN_DIMS)
        o_ref[:, sl] = out
        kdec = (kh * jnp.exp(bc_last[:, sl] - bc[:, sl])).astype(BF16)
        st_ref[h] = st * jnp.exp(bc_last[:, sl]) + _dot(ib, kdec, TN_DIMS)


def _hgrn(proj, la, l1, oml, st0, L):
    B, T, _ = proj.shape
    nc = T // L
    wide = lambda blk: pl.BlockSpec((None, L, BRANCH_WIDTH), lambda b, c: (b, c, blk))
    rowspec = pl.BlockSpec((1, BRANCH_WIDTH), lambda b, c: (0, 0))
    state = pl.BlockSpec((None, N_HEADS, HEAD_DIM, HEAD_DIM), lambda b, c: (b, 0, 0, 0))
    return pl.pallas_call(
        functools.partial(_hgrn_kernel, L=L),
        grid=(B, nc),
        in_specs=[wide(C_HGQ // 4), wide(C_HGF // 4), wide(C_HGI // 4), rowspec, rowspec, rowspec, state],
        out_specs=[pl.BlockSpec((None, L, BRANCH_WIDTH), lambda b, c: (b, c, 0)), state],
        out_shape=[
            jax.ShapeDtypeStruct((B, T, BRANCH_WIDTH), F32),
            jax.ShapeDtypeStruct((B, N_HEADS, HEAD_DIM, HEAD_DIM), F32),
        ],
        compiler_params=pltpu.CompilerParams(
            dimension_semantics=("parallel", "arbitrary"), vmem_limit_bytes=VMEM_LIMIT),
        name="hgrn2",
    )(proj, proj, proj, la, l1, oml, st0)


SB_BLOCK = 128


def _sb_kernel(*refs, tq, tk, n_past_blocks):
    if n_past_blocks:
        q_ref, kc_ref, vc_ref, kp_ref, vp_ref, o_ref = refs
    else:
        q_ref, kc_ref, vc_ref, o_ref = refs
    i = pl.program_id(2)
    scale = HEAD_DIM ** -0.5
    qb = q_ref[...].astype(BF16)

    def strict_upper(n):
        r = lax.broadcasted_iota(jnp.int32, (n, n), 0)
        cc = lax.broadcasted_iota(jnp.int32, (n, n), 1)
        return jnp.where(r > cc, 1.0, 0.0).astype(BF16)

    def block(kb, vb, carry, acc, mask, upper):
        z = _dot(qb, kb.astype(BF16), NT_DIMS) * scale
        sp = jnp.maximum(z, 0.0) + jnp.log1p(jnp.exp(-jnp.abs(z)))
        lp = -sp if mask is None else jnp.where(mask, -sp, 0.0)
        lp_hi = lp.astype(BF16)
        lp_lo = (lp - lp_hi.astype(F32)).astype(BF16)
        suffix = _dot(lp_hi, upper) + _dot(lp_lo, upper)
        a = jnp.exp((z - sp) + suffix + carry)
        if mask is not None:
            a = jnp.where(mask, a, 0.0)
        acc = acc + _dot(a.astype(BF16), vb.astype(BF16))
        carry = carry + jnp.sum(lp, axis=-1, keepdims=True)
        return carry, acc

    r = lax.broadcasted_iota(jnp.int32, (tq, tq), 0)
    cc = lax.broadcasted_iota(jnp.int32, (tq, tq), 1)
    start = pl.multiple_of(i * tq, tq)
    carry, acc = block(kc_ref[pl.ds(start, tq), :], vc_ref[pl.ds(start, tq), :],
                       jnp.zeros((tq, 1), F32), jnp.zeros((tq, HEAD_DIM), F32), cc < r, strict_upper(tq))

    upper = strict_upper(tk)
    if n_past_blocks:
        k_src, v_src, n_blocks = kp_ref, vp_ref, n_past_blocks
    else:
        k_src, v_src, n_blocks = kc_ref, vc_ref, i

    def body(t, state):
        j = n_blocks - 1 - t
        off = pl.multiple_of(j * tk, tk)
        return block(k_src[pl.ds(off, tk), :], v_src[pl.ds(off, tk), :], state[0], state[1], None, upper)

    carry, acc = lax.fori_loop(0, n_blocks, body, (carry, acc))
    o_ref[...] = acc


def _sb_attention(proj, k_past, v_past):
    B, T, _ = proj.shape
    if k_past is None:
        tq = tk = SB_BLOCK
        nq = T // tq
        kv_len = T
        past_specs, past_args, n_past_blocks = [], [], 0
    else:
        tq, tk, nq, kv_len = T, SB_BLOCK, 1, T
        P = k_past.shape[2]
        n_past_blocks = P // tk
        spec = pl.BlockSpec((None, None, P, HEAD_DIM), lambda b, h, i: (b, h, 0, 0))
        past_specs, past_args = [spec, spec], [k_past, v_past]
    return pl.pallas_call(
        functools.partial(_sb_kernel, tq=tq, tk=tk, n_past_blocks=n_past_blocks),
        grid=(B, N_HEADS, nq),
        in_specs=[
            pl.BlockSpec((None, tq, HEAD_DIM), lambda b, h, i: (b, i, C_SBQ + h)),
            pl.BlockSpec((None, kv_len, HEAD_DIM), lambda b, h, i: (b, 0, C_SBK + h)),
            pl.BlockSpec((None, kv_len, HEAD_DIM), lambda b, h, i: (b, 0, C_SBV + h)),
        ] + past_specs,
        out_specs=pl.BlockSpec((None, tq, HEAD_DIM), lambda b, h, i: (b, i, h)),
        out_shape=jax.ShapeDtypeStruct((B, T, BRANCH_WIDTH), F32),
        compiler_params=pltpu.CompilerParams(
            dimension_semantics=("parallel", "parallel", "arbitrary"), vmem_limit_bytes=VMEM_LIMIT),
        name="stickbreak",
    )(proj, proj, proj, *past_args)


def _merge_kernel(hm_ref, os_ref, oh_ref, mlo_ref, mlz_ref, sbz_ref, hgz_ref, gml_ref, gsb_ref, ghg_ref,
                  x_ref, p_ref, gmlh_ref, ghgh_ref, wb_ref, wo_ref, gpost_ref, wpg_ref, wpp_ref, r_ref):
    def head_norm(a):
        parts = []
        for h in range(N_HEADS):
            seg = a[:, h * HEAD_DIM:(h + 1) * HEAD_DIM]
            parts.append(seg * lax.rsqrt(jnp.mean(seg * seg, axis=-1, keepdims=True) + RMS_EPS))
        return jnp.concatenate(parts, axis=-1)

    silu = lambda a: a * _sigmoid(a)
    h_ml = head_norm(hm_ref[...]) * gmlh_ref[...] * _sigmoid(mlo_ref[...]) * silu(mlz_ref[...])
    h_sb = os_ref[...] * silu(sbz_ref[...])
    h_hg = head_norm(oh_ref[...]) * ghgh_ref[...] * silu(hgz_ref[...])
    merged = (_sigmoid(gml_ref[...]) * _dot(h_ml.astype(BF16), wb_ref[0])
              + _sigmoid(gsb_ref[...]) * _dot(h_sb.astype(BF16), wb_ref[1])
              + _sigmoid(ghg_ref[...]) * _dot(h_hg.astype(BF16), wb_ref[2]))
    y = _dot(merged.astype(BF16), wo_ref[...])
    r = x_ref[...] + y * lax.rsqrt(jnp.mean(y * y, axis=-1, keepdims=True) + RMS_EPS) * gpost_ref[...]
    gate = _sigmoid(_dot(r.astype(BF16), wpg_ref[...]))
    r_ref[...] = r + gate * _dot(p_ref[...].astype(BF16), wpp_ref[...])


def _merge(hm, o_s, o_h, proj2d, x2d, p2d, g_mlh, g_hgh, w_branch, w_out, g_post, w_pg, w_pp):
    n = x2d.shape[0]
    tm = min(n, 256)
    rows = lambda width, blk: pl.BlockSpec((tm, width), lambda i: (i, blk))
    full = lambda a: pl.BlockSpec(a.shape, lambda i: (0,) * a.ndim)
    params = [g_mlh, g_hgh, w_branch, w_out, g_post, w_pg, w_pp]
    return pl.pallas_call(
        _merge_kernel,
        grid=(n // tm,),
        in_specs=[
            rows(BRANCH_WIDTH, 0), rows(BRANCH_WIDTH, 0), rows(BRANCH_WIDTH, 0),
            rows(BRANCH_WIDTH, C_MLO // 4), rows(BRANCH_WIDTH, C_MLZ // 4),
            rows(BRANCH_WIDTH, C_SBZ // 4), rows(BRANCH_WIDTH, C_HGZ // 4),
            rows(D_MODEL, C_GATES // 8), rows(D_MODEL, C_GATES // 8 + 1), rows(D_MODEL, C_GATES // 8 + 2),
            rows(D_MODEL, 0), rows(PLE_DIM, 0),
        ] + [full(a) for a in params],
        out_specs=rows(D_MODEL, 0),
        out_shape=jax.ShapeDtypeStruct((n, D_MODEL), F32),
        compiler_params=pltpu.CompilerParams(
            dimension_semantics=("parallel",), vmem_limit_bytes=VMEM_LIMIT),
        name="merge",
    )(hm, o_s, o_h, proj2d, proj2d, proj2d, proj2d, proj2d, proj2d, proj2d, x2d, p2d, *params)


def _split_heads(a):
    B, T, _ = a.shape
    return a.reshape(B, T, N_HEADS, HEAD_DIM).transpose(0, 2, 1, 3)


def _layer(x, p, c0, n0, m0, conv0, s0, k_past, v_past, lw):
    B, T, _ = x.shape
    L = min(CHUNK, T)
    x2d = x.reshape(B * T, D_MODEL)
    proj2d = _inproj(x2d, lw["g_pre"], lw["w_in"])
    proj = proj2d.reshape(B, T, N_PROJ)

    conv0_pad = jnp.pad(conv0, ((0, 0), (HALO - (CONV_WIDTH - 1), 0), (0, 0)))
    hm, c_new, n_new, m_new = _mlstm(
        proj, conv0_pad, lw["w_conv"], lw["bias_row"], c0, n0.reshape(B, N_HEADS, 1, HEAD_DIM),
        jnp.broadcast_to(m0[:, :, None, None], (B, N_HEADS, 1, LANE)), L)
    o_s = _sb_attention(proj, k_past, v_past)
    o_h, st_new = _hgrn(proj, lw["log_lb"], lw["log1m_lb"], lw["one_m_lb"], jnp.swapaxes(s0, -1, -2), L)

    r = _merge(hm.reshape(B * T, BRANCH_WIDTH), o_s.reshape(B * T, BRANCH_WIDTH), o_h.reshape(B * T, BRANCH_WIDTH),
               proj2d, x2d, p.reshape(B * T, PLE_DIM), lw["g_mlh"], lw["g_hgh"], lw["w_branch"], lw["w_out"],
               lw["g_post"], lw["w_pg"], lw["w_pp"])

    col = lambda blk, width: proj[:, :, blk * LANE:blk * LANE + width]
    conv_new = col(C_MLQ, 2 * BRANCH_WIDTH)[:, T - (CONV_WIDTH - 1):, :]
    k_s = _split_heads(col(C_SBK, BRANCH_WIDTH))
    v_s = _split_heads(col(C_SBV, BRANCH_WIDTH))
    states = (c_new, n_new[:, :, 0, :], m_new[:, :, 0, 0], conv_new, k_s, v_s, jnp.swapaxes(st_new, -1, -2))
    return r.reshape(B, T, D_MODEL), states


def _layer_weights(l, lower_bounds, g_pre, w_in, b_i, b_f, w_conv, g_mlh, g_hgh, w_branch, w_out, g_post, w_pg, w_pp):
    w = w_in[l]
    n_lead = 5 * BRANCH_WIDTH
    gate_cols = jnp.pad(w[:, n_lead:n_lead + 2 * N_HEADS], ((0, 0), (0, N_GATE_PAD - 2 * N_HEADS)))
    w_re = jnp.concatenate([gate_cols, w[:, :n_lead], w[:, n_lead + 2 * N_HEADS:]], axis=1).astype(BF16)
    bias_row = jnp.pad(jnp.concatenate([b_i[l], b_f[l]]), (0, LANE - 2 * N_HEADS)).reshape(1, LANE)
    lb = lower_bounds[l].reshape(1, BRANCH_WIDTH)
    return dict(
        g_pre=g_pre[l], w_in=w_re, bias_row=bias_row, w_conv=w_conv[l],
        g_mlh=g_mlh[l].reshape(1, BRANCH_WIDTH), g_hgh=g_hgh[l].reshape(1, BRANCH_WIDTH),
        log_lb=jnp.log(lb), log1m_lb=jnp.log1p(-lb), one_m_lb=1.0 - lb,
        w_branch=w_branch[l].astype(BF16), w_out=w_out[l].astype(BF16), g_post=g_post[l].reshape(1, D_MODEL),
        w_pg=w_pg[l].astype(BF16), w_pp=w_pp[l].astype(BF16))


def kernel(x_prompt, x_sample, p_prompt, p_sample, state_mlstm_C, state_mlstm_n, state_mlstm_m, state_mlstm_conv,
           cache_sb_k, cache_sb_v, state_hgrn_S, g_pre, w_in, b_mlstm_i, b_mlstm_f, w_mlstm_conv, g_mlstm_head,
           hgrn_lb_logits, g_hgrn_head, w_branch, w_out, g_post, w_ple_gate, w_ple_proj):
    depth = w_in.shape[0]
    lb_cum = jnp.cumsum(jax.nn.softmax(hgrn_lb_logits.astype(F32), axis=0), axis=0)
    lower_bounds = lb_cum - lb_cum[:1]
    B = x_prompt.shape[0]
    yp, ys = x_prompt, x_sample
    new_p, new_s = [], []
    for l in range(depth):
        lw = _layer_weights(l, lower_bounds, g_pre, w_in, b_mlstm_i, b_mlstm_f, w_mlstm_conv, g_mlstm_head,
                            g_hgrn_head, w_branch, w_out, g_post, w_ple_gate, w_ple_proj)
        zeros = lambda *shape: jnp.zeros(shape, F32)
        yp, st = _layer(yp, p_prompt[l], zeros(B, N_HEADS, HEAD_DIM, HEAD_DIM), zeros(B, N_HEADS, HEAD_DIM),
                        zeros(B, N_HEADS), zeros(B, CONV_WIDTH - 1, 2 * BRANCH_WIDTH),
                        zeros(B, N_HEADS, HEAD_DIM, HEAD_DIM), None, None, lw)
        new_p.append(st)
        ys, st = _layer(ys, p_sample[l], state_mlstm_C[l], state_mlstm_n[l], state_mlstm_m[l], state_mlstm_conv[l],
                        state_hgrn_S[l], cache_sb_k[l], cache_sb_v[l], lw)
        new_s.append(st)
    stack = lambda sts: [jnp.stack([s[i] for s in sts]) for i in range(7)]
    return (yp, ys, *stack(new_p), *stack(new_s))
```

```python
import functools

import jax
import jax.numpy as jnp
import numpy as np
from jax import lax
from jax.experimental import pallas as pl
from jax.experimental.pallas import tpu as pltpu

F32 = jnp.float32
BF16 = jnp.bfloat16

D_MODEL = 1024
PLE_DIM = 256
N_HEADS = 4
BRANCH_WIDTH = D_MODEL // 2
HEAD_DIM = BRANCH_WIDTH // N_HEADS
CONV_WIDTH = 4
RMS_EPS = 1e-6
CHUNK = 256
HALO = 8
NEG_BIG = -1e30

N_GATE_PAD = BRANCH_WIDTH
N_PROJ = N_GATE_PAD + 13 * BRANCH_WIDTH + 3 * D_MODEL
LANE = 128
C_IF, C_MLQ, C_MLK, C_MLV, C_MLO, C_MLZ = 0, 4, 8, 12, 16, 20
C_SBQ, C_SBK, C_SBV, C_SBZ = 24, 28, 32, 36
C_HGQ, C_HGF, C_HGI, C_HGZ = 40, 44, 48, 52
C_GATES = 56

VMEM_LIMIT = 56 * 1024 * 1024

NT_DIMS = (((1,), (1,)), ((), ()))
TN_DIMS = (((0,), (0,)), ((), ()))


def _sigmoid(x):
    return 1.0 / (1.0 + jnp.exp(-x))


def _log_sigmoid(x):
    return jnp.minimum(x, 0.0) - jnp.log1p(jnp.exp(-jnp.abs(x)))


def _dot(a, b, dims=None, precision=None):
    if dims is None:
        dims = (((a.ndim - 1,), (0,)), ((), ()))
    return lax.dot_general(a, b, dims, precision=precision, preferred_element_type=F32)


def _dot_exact(a, b, dims=None):
    return _dot(a, b, dims, precision=lax.Precision.HIGHEST)


def _inproj_kernel(x_ref, g_ref, w_ref, o_ref, u_ref):
    @pl.when(pl.program_id(1) == 0)
    def _():
        x = x_ref[...]
        ms = jnp.mean(x * x, axis=-1, keepdims=True)
        u_ref[...] = (x * lax.rsqrt(ms + RMS_EPS) * g_ref[...]).astype(BF16)

    o_ref[...] = jnp.dot(u_ref[...], w_ref[...], preferred_element_type=F32)


def _inproj(x2d, g_pre, w_bf16):
    n = x2d.shape[0]
    tm = min(n, 1024)
    tn = 1024
    return pl.pallas_call(
        _inproj_kernel,
        grid=(n // tm, N_PROJ // tn),
        in_specs=[
            pl.BlockSpec((tm, D_MODEL), lambda i, j: (i, 0)),
            pl.BlockSpec((1, D_MODEL), lambda i, j: (0, 0)),
            pl.BlockSpec((D_MODEL, tn), lambda i, j: (0, j)),
        ],
        out_specs=pl.BlockSpec((tm, tn), lambda i, j: (i, j)),
        out_shape=jax.ShapeDtypeStruct((n, N_PROJ), F32),
        scratch_shapes=[pltpu.VMEM((tm, D_MODEL), BF16)],
        compiler_params=pltpu.CompilerParams(
            dimension_semantics=("parallel", "arbitrary"), vmem_limit_bytes=VMEM_LIMIT),
        name="inproj",
    )(x2d, g_pre.reshape(1, D_MODEL), w_bf16)


def _mlstm_kernel(q_ref, k_ref, v_ref, gi_ref, conv0_ref, wc_ref, bias_ref, c0_ref, n0_ref, m0_ref,
                  hm_ref, c_ref, n_ref, m_ref, xs_ref, *, L):
    c = pl.program_id(1)

    @pl.when(c == 0)
    def _():
        c_ref[...] = c0_ref[...]
        n_ref[...] = n0_ref[...]
        m_ref[...] = m0_ref[...]
        xs_ref[0:HALO, :] = conv0_ref[...]

    xs_ref[HALO:HALO + L, 0:BRANCH_WIDTH] = q_ref[...]
    xs_ref[HALO:HALO + L, BRANCH_WIDTH:] = k_ref[...]
    base = HALO - (CONV_WIDTH - 1)
    qk = xs_ref[base:base + L, :] * wc_ref[0:1, :]
    for j in range(1, CONV_WIDTH):
        qk = qk + xs_ref[base + j:base + j + L, :] * wc_ref[j:j + 1, :]
    qk = qk * _sigmoid(qk)
    xs_ref[0:HALO, :] = xs_ref[L:L + HALO, :]

    gates = gi_ref[...] + bias_ref[...]
    row = lax.broadcasted_iota(jnp.int32, (L, L), 0)
    col = lax.broadcasted_iota(jnp.int32, (L, L), 1)
    causal = col <= row
    ltri = jnp.where(causal, 1.0, 0.0).astype(F32)
    bcum = _dot_exact(ltri, _log_sigmoid(gates))
    lane = lax.broadcasted_iota(jnp.int32, (L, LANE), 1)
    scale = HEAD_DIM ** -0.5

    for h in range(N_HEADS):
        ig = gates[:, h:h + 1]
        b = bcum[:, N_HEADS + h:N_HEADS + h + 1]
        lhs = jnp.where(lane == 0, b, jnp.where(lane == 1, 1.0, 0.0))
        rhs = jnp.where(lane == 0, 1.0, jnp.where(lane == 1, ig - b, 0.0))
        dmat = jnp.where(causal, _dot_exact(lhs, rhs, NT_DIMS), NEG_BIG)
        m_prev = m_ref[h][:, 0:1]
        inter = b + m_prev
        m_t = jnp.maximum(jnp.max(dmat, axis=-1, keepdims=True), inter)
        qh = qk[:, h * HEAD_DIM:(h + 1) * HEAD_DIM]
        kh = qk[:, BRANCH_WIDTH + h * HEAD_DIM:BRANCH_WIDTH + (h + 1) * HEAD_DIM] * scale
        vh = v_ref[:, h * HEAD_DIM:(h + 1) * HEAD_DIM]
        qb, kb, vb = qh.astype(BF16), kh.astype(BF16), vh.astype(BF16)
        s = _dot(qb, kb, NT_DIMS) * jnp.exp(dmat - m_t)
        w_inter = jnp.exp(inter - m_t)
        c_prev = c_ref[h]
        n_prev = n_ref[h]
        num = _dot(s.astype(BF16), vb) + w_inter * _dot(qb, c_prev.astype(BF16))
        den = jnp.sum(s, axis=-1, keepdims=True) + w_inter * jnp.sum(qh * n_prev, axis=-1, keepdims=True)
        hm_ref[:, h * HEAD_DIM:(h + 1) * HEAD_DIM] = num / jnp.maximum(jnp.abs(den), jnp.exp(-m_t))
        m_new = m_t[L - 1:L, :]
        b_last = b[L - 1:L, :]
        dec_s = jnp.exp(b_last - b + ig - m_new)
        dec_c = jnp.exp(b_last + m_prev - m_new)
        kd = kh * dec_s
        c_ref[h] = dec_c * c_prev + _dot(kd.astype(BF16), vb, TN_DIMS)
        n_ref[h] = dec_c * n_prev + jnp.sum(kd, axis=0, keepdims=True)
        m_ref[h] = jnp.broadcast_to(m_new, (1, LANE))


def _mlstm(proj, conv0_pad, w_conv, bias_row, c0, n0, m0, L):
    B, T, _ = proj.shape
    nc = T // L
    wide = lambda blk: pl.BlockSpec((None, L, BRANCH_WIDTH), lambda b, c: (b, c, blk))
    state = lambda shp: pl.BlockSpec((None,) + shp, lambda b, c: (b,) + (0,) * len(shp))
    return pl.pallas_call(
        functools.partial(_mlstm_kernel, L=L),
        grid=(B, nc),
        in_specs=[
            wide(C_MLQ // 4), wide(C_MLK // 4), wide(C_MLV // 4),
            pl.BlockSpec((None, L, LANE), lambda b, c: (b, c, C_IF)),
            state((HALO, D_MODEL)),
            pl.BlockSpec((CONV_WIDTH, D_MODEL), lambda b, c: (0, 0)),
            pl.BlockSpec((1, LANE), lambda b, c: (0, 0)),
            state((N_HEADS, HEAD_DIM, HEAD_DIM)), state((N_HEADS, 1, HEAD_DIM)), state((N_HEADS, 1, LANE)),
        ],
        out_specs=[
            pl.BlockSpec((None, L, BRANCH_WIDTH), lambda b, c: (b, c, 0)),
            state((N_HEADS, HEAD_DIM, HEAD_DIM)), state((N_HEADS, 1, HEAD_DIM)), state((N_HEADS, 1, LANE)),
        ],
        out_shape=[
            jax.ShapeDtypeStruct((B, T, BRANCH_WIDTH), F32),
            jax.ShapeDtypeStruct((B, N_HEADS, HEAD_DIM, HEAD_DIM), F32),
            jax.ShapeDtypeStruct((B, N_HEADS, 1, HEAD_DIM), F32),
            jax.ShapeDtypeStruct((B, N_HEADS, 1, LANE), F32),
        ],
        scratch_shapes=[pltpu.VMEM((L + HALO, D_MODEL), F32)],
        compiler_params=pltpu.CompilerParams(
            dimension_semantics=("parallel", "arbitrary"), vmem_limit_bytes=VMEM_LIMIT),
        name="mlstm",
    )(proj, proj, proj, proj, conv0_pad, w_conv, bias_row, c0, n0, m0)


SUBLANES = 8


def _split3_bf16(x):
    hi = x.astype(BF16)
    r = x - hi.astype(F32)
    mid = r.astype(BF16)
    lo = (r - mid.astype(F32)).astype(BF16)
    return hi, mid, lo


def _hgrn_level_operand(lv, bc, g, kk, q_all, L):
    width = bc.shape[-1]
    rowi = lax.broadcasted_iota(jnp.int32, (L, width), 0)
    first_half = (rowi // lv) % 2 == 0
    if lv == 1:
        e = jnp.where(first_half, 0.0, g)
    elif 2 * lv >= SUBLANES:
        b3 = bc.reshape(L // (2 * lv), 2 * lv, width)
        e = (b3 - b3[:, lv - 1:lv, :]).reshape(L, width)
    else:
        b3 = bc.reshape(L // SUBLANES, SUBLANES, width)
        sub = lax.broadcasted_iota(jnp.int32, b3.shape, 1)
        rho = b3[:, lv - 1:lv, :]
        for start in range(2 * lv, SUBLANES, 2 * lv):
            rho = jnp.where(sub >= start, b3[:, start + lv - 1:start + lv, :], rho)
        e = (b3 - rho).reshape(L, width)
    return (jnp.where(first_half, kk, q_all) * jnp.exp(-jnp.abs(e))).astype(BF16)


def _hgrn_kernel(q_ref, f_ref, i_ref, la_ref, l1_ref, oml_ref, lvl_ref, s0_ref, o_ref, st_ref, *, L):
    c = pl.program_id(1)

    @pl.when(c == 0)
    def _():
        st_ref[...] = s0_ref[...]

    f = f_ref[...]
    la = la_ref[...]
    lb_term = l1_ref[...] + _log_sigmoid(f)
    g = jnp.maximum(la, lb_term) + jnp.log1p(jnp.exp(-jnp.abs(la - lb_term)))
    kk = oml_ref[...] / (1.0 + jnp.exp(f))
    q_all = q_ref[...]

    row = lax.broadcasted_iota(jnp.int32, (L, L), 0)
    col = lax.broadcasted_iota(jnp.int32, (L, L), 1)
    ltri = jnp.where(col <= row, 1.0, 0.0).astype(BF16)
    bc = sum(_dot(ltri, piece) for piece in _split3_bf16(g))
    bc_last = bc[L - 1:L, :]
    qe = (q_all * jnp.exp(bc)).astype(BF16)
    kdec = (kk * jnp.exp(bc_last - bc)).astype(BF16)
    dec_state = jnp.exp(bc_last)
    n_levels = L.bit_length() - 1
    operands = [_hgrn_level_operand(1 << p, bc, g, kk, q_all, L) for p in range(n_levels)]
    lvl = lvl_ref[...]

    for h in range(N_HEADS):
        sl = slice(h * HEAD_DIM, (h + 1) * HEAD_DIM)
        ih = i_ref[:, sl]
        ib = ih.astype(BF16)
        st = st_ref[h]
        a = jnp.zeros((L, L), F32)
        for p, y in enumerate(operands):
            a = jnp.where(lvl == p, _dot(y[:, sl], y[:, sl], NT_DIMS), a)
        diag = jnp.sum(q_all[:, sl] * kk[:, sl], axis=-1, keepdims=True)
        o_ref[:, sl] = _dot(a.astype(BF16), ib) + _dot(qe[:, sl], st.astype(BF16), NT_DIMS) + diag * ih
        st_ref[h] = st * dec_state[:, sl] + _dot(ib, kdec[:, sl], TN_DIMS)


def _hgrn_level_index(L):
    t = np.arange(L)[:, None]
    s = np.arange(L)[None, :]
    x = np.maximum(t ^ s, 1)
    return np.where(s < t, np.floor(np.log2(x)), -1).astype(np.int32)


def _hgrn(proj, la, l1, oml, st0, L):
    B, T, _ = proj.shape
    nc = T // L
    assert L & (L - 1) == 0 and L >= SUBLANES
    wide = lambda blk: pl.BlockSpec((None, L, BRANCH_WIDTH), lambda b, c: (b, c, blk))
    rowspec = pl.BlockSpec((1, BRANCH_WIDTH), lambda b, c: (0, 0))
    state = pl.BlockSpec((None, N_HEADS, HEAD_DIM, HEAD_DIM), lambda b, c: (b, 0, 0, 0))
    return pl.pallas_call(
        functools.partial(_hgrn_kernel, L=L),
        grid=(B, nc),
        in_specs=[wide(C_HGQ // 4), wide(C_HGF // 4), wide(C_HGI // 4), rowspec, rowspec, rowspec,
                  pl.BlockSpec((L, L), lambda b, c: (0, 0)), state],
        out_specs=[pl.BlockSpec((None, L, BRANCH_WIDTH), lambda b, c: (b, c, 0)), state],
        out_shape=[
            jax.ShapeDtypeStruct((B, T, BRANCH_WIDTH), F32),
            jax.ShapeDtypeStruct((B, N_HEADS, HEAD_DIM, HEAD_DIM), F32),
        ],
        compiler_params=pltpu.CompilerParams(
            dimension_semantics=("parallel", "arbitrary"), vmem_limit_bytes=VMEM_LIMIT),
        name="hgrn2",
    )(proj, proj, proj, la, l1, oml, jnp.asarray(_hgrn_level_index(L)), st0)


SB_KEY_BLOCK = 256
SB_QUERY_BLOCK = 512
SB_ZERO_WEIGHT_LOG = -104.0


def _strict_upper(n):
    r = lax.broadcasted_iota(jnp.int32, (n, n), 0)
    c = lax.broadcasted_iota(jnp.int32, (n, n), 1)
    return jnp.where(r > c, 1.0, 0.0).astype(BF16)


def _sb_block(qb, kb, vb, carry, acc, upper, key_offset=None):
    z = _dot(qb, kb, NT_DIMS)
    sp = jnp.maximum(z, 0.0) + jnp.log1p(jnp.exp(-jnp.abs(z)))
    if key_offset is None:
        mask = None
        lp = -sp
    else:
        r = lax.broadcasted_iota(jnp.int32, z.shape, 0)
        c = lax.broadcasted_iota(jnp.int32, z.shape, 1)
        mask = c + key_offset < r
        lp = jnp.where(mask, -sp, 0.0)
    lp_hi = lp.astype(BF16)
    lp_lo = (lp - lp_hi.astype(F32)).astype(BF16)
    suffix = _dot(lp_hi, upper) + _dot(lp_lo, upper)
    a = jnp.exp((z - sp) + suffix + carry)
    if mask is not None:
        a = jnp.where(mask, a, 0.0)
    acc = acc + _dot(a.astype(BF16), vb)
    carry = carry + jnp.sum(lp, axis=-1, keepdims=True)
    return carry, acc


def _sb_walk(qb, carry, acc, n_blocks, load_block, tk):
    upper = _strict_upper(tk)

    def cond(state):
        t, carry, _ = state
        return jnp.logical_and(t < n_blocks, jnp.max(carry) > SB_ZERO_WEIGHT_LOG)

    def body(state):
        t, carry, acc = state
        kb, vb = load_block(n_blocks - 1 - t)
        carry, acc = _sb_block(qb, kb, vb, carry, acc, upper)
        return t + 1, carry, acc

    _, carry, acc = lax.while_loop(cond, body, (jnp.int32(0), carry, acc))
    return acc


def _sb_prompt_kernel(q_ref, k_ref, v_ref, o_ref, kb_ref, vb_ref, *, tq, tk):
    i = pl.program_id(2)
    T = k_ref.shape[0]
    cast_rows = min(T, 1024)

    @pl.when(i == 0)
    def _():
        def cast(t, _):
            rows = pl.ds(pl.multiple_of(t * cast_rows, cast_rows), cast_rows)
            kb_ref[rows, :] = k_ref[rows, :].astype(BF16)
            vb_ref[rows, :] = v_ref[rows, :].astype(BF16)
            return 0
        lax.fori_loop(0, T // cast_rows, cast, 0)

    qb = (q_ref[...] * (HEAD_DIM ** -0.5)).astype(BF16)
    carry = jnp.zeros((tq, 1), F32)
    acc = jnp.zeros((tq, HEAD_DIM), F32)
    upper = _strict_upper(tk)
    base = i * tq
    for d in reversed(range(tq // tk)):
        rows = pl.ds(pl.multiple_of(base + d * tk, tk), tk)
        carry, acc = _sb_block(qb, kb_ref[rows, :], vb_ref[rows, :], carry, acc, upper, key_offset=d * tk)

    def load_block(j):
        rows = pl.ds(pl.multiple_of(j * tk, tk), tk)
        return kb_ref[rows, :], vb_ref[rows, :]

    o_ref[...] = _sb_walk(qb, carry, acc, i * (tq // tk), load_block, tk)


def _sb_sample_kernel(q_ref, kc_ref, vc_ref, kp_ref, vp_ref, o_ref, *, tk):
    tq = q_ref.shape[0]
    qb = (q_ref[...] * (HEAD_DIM ** -0.5)).astype(BF16)
    carry, acc = _sb_block(qb, kc_ref[...].astype(BF16), vc_ref[...].astype(BF16), jnp.zeros((tq, 1), F32),
                           jnp.zeros((tq, HEAD_DIM), F32), _strict_upper(tq), key_offset=0)

    def load_block(j):
        rows = pl.ds(pl.multiple_of(j * tk, tk), tk)
        return kp_ref[rows, :].astype(BF16), vp_ref[rows, :].astype(BF16)

    o_ref[...] = _sb_walk(qb, carry, acc, kp_ref.shape[0] // tk, load_block, tk)


def _sb_attention(proj, k_past, v_past):
    B, T, _ = proj.shape
    col = lambda blk: pl.BlockSpec((None, T, HEAD_DIM), lambda b, h, i: (b, 0, blk + h))
    if k_past is None:
        tq = min(T, SB_QUERY_BLOCK)
        kernel_fn = functools.partial(_sb_prompt_kernel, tq=tq, tk=min(tq, SB_KEY_BLOCK))
        past_specs, past_args = [], []
        scratch = [pltpu.VMEM((T, HEAD_DIM), BF16), pltpu.VMEM((T, HEAD_DIM), BF16)]
    else:
        tq = T
        P = k_past.shape[2]
        kernel_fn = functools.partial(_sb_sample_kernel, tk=min(P, SB_KEY_BLOCK))
        spec = pl.BlockSpec((None, None, P, HEAD_DIM), lambda b, h, i: (b, h, 0, 0))
        past_specs, past_args = [spec, spec], [k_past, v_past]
        scratch = []
    return pl.pallas_call(
        kernel_fn,
        grid=(B, N_HEADS, T // tq),
        in_specs=[pl.BlockSpec((None, tq, HEAD_DIM), lambda b, h, i: (b, i, C_SBQ + h)), col(C_SBK), col(C_SBV)]
        + past_specs,
        out_specs=pl.BlockSpec((None, tq, HEAD_DIM), lambda b, h, i: (b, i, h)),
        out_shape=jax.ShapeDtypeStruct((B, T, BRANCH_WIDTH), F32),
        scratch_shapes=scratch,
        compiler_params=pltpu.CompilerParams(
            dimension_semantics=("parallel", "parallel", "arbitrary"), vmem_limit_bytes=VMEM_LIMIT),
        name="stickbreak",
    )(proj, proj, proj, *past_args)


def _merge_kernel(hm_ref, os_ref, oh_ref, mlo_ref, mlz_ref, sbz_ref, hgz_ref, gml_ref, gsb_ref, ghg_ref,
                  x_ref, p_ref, gmlh_ref, ghgh_ref, wb_ref, wo_ref, gpost_ref, wpg_ref, wpp_ref, r_ref):
    def head_norm(a):
        parts = []
        for h in range(N_HEADS):
            seg = a[:, h * HEAD_DIM:(h + 1) * HEAD_DIM]
            parts.append(seg * lax.rsqrt(jnp.mean(seg * seg, axis=-1, keepdims=True) + RMS_EPS))
        return jnp.concatenate(parts, axis=-1)

    silu = lambda a: a * _sigmoid(a)
    h_ml = head_norm(hm_ref[...]) * gmlh_ref[...] * _sigmoid(mlo_ref[...]) * silu(mlz_ref[...])
    h_sb = os_ref[...] * silu(sbz_ref[...])
    h_hg = head_norm(oh_ref[...]) * ghgh_ref[...] * silu(hgz_ref[...])
    merged = (_sigmoid(gml_ref[...]) * _dot(h_ml.astype(BF16), wb_ref[0])
              + _sigmoid(gsb_ref[...]) * _dot(h_sb.astype(BF16), wb_ref[1])
              + _sigmoid(ghg_ref[...]) * _dot(h_hg.astype(BF16), wb_ref[2]))
    y = _dot(merged.astype(BF16), wo_ref[...])
    r = x_ref[...] + y * lax.rsqrt(jnp.mean(y * y, axis=-1, keepdims=True) + RMS_EPS) * gpost_ref[...]
    gate = _sigmoid(_dot(r.astype(BF16), wpg_ref[...]))
    r_ref[...] = r + gate * _dot(p_ref[...].astype(BF16), wpp_ref[...])


def _merge(hm, o_s, o_h, proj2d, x2d, p2d, g_mlh, g_hgh, w_branch, w_out, g_post, w_pg, w_pp):
    n = x2d.shape[0]
    tm = min(n, 256)
    rows = lambda width, blk: pl.BlockSpec((tm, width), lambda i: (i, blk))
    full = lambda a: pl.BlockSpec(a.shape, lambda i: (0,) * a.ndim)
    params = [g_mlh, g_hgh, w_branch, w_out, g_post, w_pg, w_pp]
    return pl.pallas_call(
        _merge_kernel,
        grid=(n // tm,),
        in_specs=[
            rows(BRANCH_WIDTH, 0), rows(BRANCH_WIDTH, 0), rows(BRANCH_WIDTH, 0),
            rows(BRANCH_WIDTH, C_MLO // 4), rows(BRANCH_WIDTH, C_MLZ // 4),
            rows(BRANCH_WIDTH, C_SBZ // 4), rows(BRANCH_WIDTH, C_HGZ // 4),
            rows(D_MODEL, C_GATES // 8), rows(D_MODEL, C_GATES // 8 + 1), rows(D_MODEL, C_GATES // 8 + 2),
            rows(D_MODEL, 0), rows(PLE_DIM, 0),
        ] + [full(a) for a in params],
        out_specs=rows(D_MODEL, 0),
        out_shape=jax.ShapeDtypeStruct((n, D_MODEL), F32),
        compiler_params=pltpu.CompilerParams(
            dimension_semantics=("parallel",), vmem_limit_bytes=VMEM_LIMIT),
        name="merge",
    )(hm, o_s, o_h, proj2d, proj2d, proj2d, proj2d, proj2d, proj2d, proj2d, x2d, p2d, *params)


def _split_heads(a):
    B, T, _ = a.shape
    return a.reshape(B, T, N_HEADS, HEAD_DIM).transpose(0, 2, 1, 3)


def _layer(x, p, c0, n0, m0, conv0, s0, k_past, v_past, lw):
    B, T, _ = x.shape
    L = min(CHUNK, T)
    x2d = x.reshape(B * T, D_MODEL)
    proj2d = _inproj(x2d, lw["g_pre"], lw["w_in"])
    proj = proj2d.reshape(B, T, N_PROJ)

    conv0_pad = jnp.pad(conv0, ((0, 0), (HALO - (CONV_WIDTH - 1), 0), (0, 0)))
    hm, c_new, n_new, m_new = _mlstm(
        proj, conv0_pad, lw["w_conv"], lw["bias_row"], c0, n0.reshape(B, N_HEADS, 1, HEAD_DIM),
        jnp.broadcast_to(m0[:, :, None, None], (B, N_HEADS, 1, LANE)), L)
    o_s = _sb_attention(proj, k_past, v_past)
    o_h, st_new = _hgrn(proj, lw["log_lb"], lw["log1m_lb"], lw["one_m_lb"], jnp.swapaxes(s0, -1, -2), L)

    r = _merge(hm.reshape(B * T, BRANCH_WIDTH), o_s.reshape(B * T, BRANCH_WIDTH), o_h.reshape(B * T, BRANCH_WIDTH),
               proj2d, x2d, p.reshape(B * T, PLE_DIM), lw["g_mlh"], lw["g_hgh"], lw["w_branch"], lw["w_out"],
               lw["g_post"], lw["w_pg"], lw["w_pp"])

    col = lambda blk, width: proj[:, :, blk * LANE:blk * LANE + width]
    conv_new = col(C_MLQ, 2 * BRANCH_WIDTH)[:, T - (CONV_WIDTH - 1):, :]
    k_s = _split_heads(col(C_SBK, BRANCH_WIDTH))
    v_s = _split_heads(col(C_SBV, BRANCH_WIDTH))
    states = (c_new, n_new[:, :, 0, :], m_new[:, :, 0, 0], conv_new, k_s, v_s, jnp.swapaxes(st_new, -1, -2))
    return r.reshape(B, T, D_MODEL), states


def _layer_weights(l, lower_bounds, g_pre, w_in, b_i, b_f, w_conv, g_mlh, g_hgh, w_branch, w_out, g_post, w_pg, w_pp):
    w = w_in[l]
    n_lead = 5 * BRANCH_WIDTH
    gate_cols = jnp.pad(w[:, n_lead:n_lead + 2 * N_HEADS], ((0, 0), (0, N_GATE_PAD - 2 * N_HEADS)))
    w_re = jnp.concatenate([gate_cols, w[:, :n_lead], w[:, n_lead + 2 * N_HEADS:]], axis=1).astype(BF16)
    bias_row = jnp.pad(jnp.concatenate([b_i[l], b_f[l]]), (0, LANE - 2 * N_HEADS)).reshape(1, LANE)
    lb = lower_bounds[l].reshape(1, BRANCH_WIDTH)
    return dict(
        g_pre=g_pre[l], w_in=w_re, bias_row=bias_row, w_conv=w_conv[l],
        g_mlh=g_mlh[l].reshape(1, BRANCH_WIDTH), g_hgh=g_hgh[l].reshape(1, BRANCH_WIDTH),
        log_lb=jnp.log(lb), log1m_lb=jnp.log1p(-lb), one_m_lb=1.0 - lb,
        w_branch=w_branch[l].astype(BF16), w_out=w_out[l].astype(BF16), g_post=g_post[l].reshape(1, D_MODEL),
        w_pg=w_pg[l].astype(BF16), w_pp=w_pp[l].astype(BF16))


def kernel(x_prompt, x_sample, p_prompt, p_sample, state_mlstm_C, state_mlstm_n, state_mlstm_m, state_mlstm_conv,
           cache_sb_k, cache_sb_v, state_hgrn_S, g_pre, w_in, b_mlstm_i, b_mlstm_f, w_mlstm_conv, g_mlstm_head,
           hgrn_lb_logits, g_hgrn_head, w_branch, w_out, g_post, w_ple_gate, w_ple_proj):
    depth = w_in.shape[0]
    lb_cum = jnp.cumsum(jax.nn.softmax(hgrn_lb_logits.astype(F32), axis=0), axis=0)
    lower_bounds = lb_cum - lb_cum[:1]
    B = x_prompt.shape[0]
    yp, ys = x_prompt, x_sample
    new_p, new_s = [], []
    for l in range(depth):
        lw = _layer_weights(l, lower_bounds, g_pre, w_in, b_mlstm_i, b_mlstm_f, w_mlstm_conv, g_mlstm_head,
                            g_hgrn_head, w_branch, w_out, g_post, w_ple_gate, w_ple_proj)
        zeros = lambda *shape: jnp.zeros(shape, F32)
        yp, st = _layer(yp, p_prompt[l], zeros(B, N_HEADS, HEAD_DIM, HEAD_DIM), zeros(B, N_HEADS, HEAD_DIM),
                        zeros(B, N_HEADS), zeros(B, CONV_WIDTH - 1, 2 * BRANCH_WIDTH),
                        zeros(B, N_HEADS, HEAD_DIM, HEAD_DIM), None, None, lw)
        new_p.append(st)
        ys, st = _layer(ys, p_sample[l], state_mlstm_C[l], state_mlstm_n[l], state_mlstm_m[l], state_mlstm_conv[l],
                        state_hgrn_S[l], cache_sb_k[l], cache_sb_v[l], lw)
        new_s.append(st)
    stack = lambda sts: [jnp.stack([s[i] for s in sts]) for i in range(7)]
    return (yp, ys, *stack(new_p), *stack(new_s))
```

```python
import functools

import jax
import jax.numpy as jnp
import numpy as np
from jax import lax
from jax.experimental import pallas as pl
from jax.experimental.pallas import tpu as pltpu

F32 = jnp.float32
BF16 = jnp.bfloat16

D_MODEL = 1024
PLE_DIM = 256
N_HEADS = 4
BRANCH_WIDTH = D_MODEL // 2
HEAD_DIM = BRANCH_WIDTH // N_HEADS
CONV_WIDTH = 4
RMS_EPS = 1e-6
LANE = 128
SUBLANES = 8
HALO = SUBLANES
NEG_BIG = -1e30
MLSTM_CHUNK = 128
HGRN_CHUNK = 128

W = BRANCH_WIDTH
SRC_ML_QKV = (0, 3 * W)
SRC_ML_OZ = (3 * W, 5 * W)
SRC_ML_I = (5 * W, 5 * W + N_HEADS)
SRC_ML_F = (5 * W + N_HEADS, 5 * W + 2 * N_HEADS)
_S = 5 * W + 2 * N_HEADS
SRC_SB_Q = (_S, _S + W)
SRC_SB_KV = (_S + W, _S + 3 * W)
SRC_SB_Z = (_S + 3 * W, _S + 4 * W)
SRC_HG_QFI = (_S + 4 * W, _S + 7 * W)
SRC_HG_Z_GATES = (_S + 7 * W, _S + 8 * W + 3 * D_MODEL)

C_IF, C_MLQ, C_MLK, C_MLV, C_SBQ, C_HGQ, C_HGF, C_HGI = 0, 4, 8, 12, 16, 20, 24, 28
N_PROJ = 32 * LANE
INPROJ_TILE = 2 * BRANCH_WIDTH
N_FORGET_COPIES = 3
G_MLO, G_MLZ, G_SBZ, G_HGZ, G_GATES = 0, W, 2 * W, 3 * W, 4 * W

VMEM_LIMIT = 56 * 1024 * 1024

NT_DIMS = (((1,), (1,)), ((), ()))
TN_DIMS = (((0,), (0,)), ((), ()))


def _sigmoid(x):
    return 1.0 / (1.0 + jnp.exp(-x))


def _log_sigmoid(x):
    return jnp.minimum(x, 0.0) - jnp.log1p(jnp.exp(-jnp.abs(x)))


def _dot(a, b, dims=None):
    if dims is None:
        dims = (((a.ndim - 1,), (0,)), ((), ()))
    return lax.dot_general(a, b, dims, preferred_element_type=F32)


def _split3(x):
    hi = x.astype(BF16).astype(F32)
    r = x - hi
    mid = r.astype(BF16).astype(F32)
    return hi, mid, r - mid


def _rms_norm_bf16(x, g):
    ms = jnp.mean(x * x, axis=-1, keepdims=True)
    return (x * lax.rsqrt(ms + RMS_EPS) * g).astype(BF16)


def _inproj_kernel(*refs, n_proj_tiles, aliased):
    x_ref, g_ref, w_ref = refs[:3]
    o_ref, k_ref, v_ref, u_ref = refs[-4:]
    j = pl.program_id(1)

    @pl.when(j == 0)
    def _():
        u_ref[...] = _rms_norm_bf16(x_ref[...], g_ref[...])

    res = jnp.dot(u_ref[...], w_ref[...], preferred_element_type=F32)

    @pl.when(j < n_proj_tiles)
    def _():
        o_ref[...] = res

    @pl.when(j == n_proj_tiles)
    def _():
        bt, _, tt, _ = k_ref.shape
        for h in range(N_HEADS):
            k_ref[:, h] = res[:, h * HEAD_DIM:(h + 1) * HEAD_DIM].reshape(bt, tt, HEAD_DIM)
            v_ref[:, h] = res[:, W + h * HEAD_DIM:W + (h + 1) * HEAD_DIM].reshape(bt, tt, HEAD_DIM)


def _inproj(l, depth, x, g_pre, w_bf16, kv_prev):
    B, T, _ = x.shape
    n = B * T
    tm = min(n, 1024)
    bt, tt = max(1, tm // T), min(tm, T)
    n_proj_tiles = N_PROJ // INPROJ_TILE
    kv_block = pl.BlockSpec((None, bt, N_HEADS, tt, HEAD_DIM),
                            lambda i, j: (l, (i * tm) // T // bt, 0, ((i * tm) % T) // tt, 0))
    kv_shape = jax.ShapeDtypeStruct((depth, B, N_HEADS, T, HEAD_DIM), F32)
    aliased = kv_prev is not None
    prev_specs = [pl.BlockSpec(memory_space=pl.ANY)] * 2 if aliased else []
    return pl.pallas_call(
        functools.partial(_inproj_kernel, n_proj_tiles=n_proj_tiles, aliased=aliased),
        grid=(n // tm, n_proj_tiles + 1),
        in_specs=[
            pl.BlockSpec((tm, D_MODEL), lambda i, j: (i, 0)),
            pl.BlockSpec((1, D_MODEL), lambda i, j: (0, 0)),
            pl.BlockSpec((D_MODEL, INPROJ_TILE), lambda i, j: (0, j)),
        ] + prev_specs,
        out_specs=[
            pl.BlockSpec((tm, INPROJ_TILE), lambda i, j: (i, jnp.minimum(j, n_proj_tiles - 1))),
            kv_block, kv_block,
        ],
        out_shape=[jax.ShapeDtypeStruct((n, N_PROJ), F32), kv_shape, kv_shape],
        input_output_aliases={3: 1, 4: 2} if aliased else {},
        scratch_shapes=[pltpu.VMEM((tm, D_MODEL), BF16)],
        compiler_params=pltpu.CompilerParams(
            dimension_semantics=("parallel", "arbitrary"), vmem_limit_bytes=VMEM_LIMIT),
        name="inproj",
    )(x.reshape(n, D_MODEL), g_pre.reshape(1, D_MODEL), w_bf16, *(kv_prev if aliased else ()))


def _mlstm_kernel(q_ref, k_ref, v_ref, gi_ref, conv0_ref, wc_ref, bias_ref, c0_ref, n0_ref, m0_ref,
                  hm_ref, c_ref, n_ref, m_ref, xs_ref, *, L):
    c = pl.program_id(1)

    @pl.when(c == 0)
    def _():
        c_ref[...] = c0_ref[...]
        n_ref[...] = n0_ref[...]
        m_ref[...] = m0_ref[...]
        xs_ref[0:HALO, :] = conv0_ref[...]

    xs_ref[HALO:HALO + L, 0:BRANCH_WIDTH] = q_ref[...]
    xs_ref[HALO:HALO + L, BRANCH_WIDTH:] = k_ref[...]
    base = HALO - (CONV_WIDTH - 1)
    qk = xs_ref[base:base + L, :] * wc_ref[0:1, :]
    for j in range(1, CONV_WIDTH):
        qk = qk + xs_ref[base + j:base + j + L, :] * wc_ref[j:j + 1, :]
    qk = qk * _sigmoid(qk)
    xs_ref[0:HALO, :] = xs_ref[L:L + HALO, :]

    gates = gi_ref[...] + bias_ref[...]
    row = lax.broadcasted_iota(jnp.int32, (L, L), 0)
    col = lax.broadcasted_iota(jnp.int32, (L, L), 1)
    causal = col <= row
    ltri = jnp.where(causal, 1.0, 0.0).astype(BF16)
    lane = lax.broadcasted_iota(jnp.int32, (L, LANE), 1)
    hi, mid, lo = _split3(_log_sigmoid(gates))
    pieces = jnp.where(lane < 2 * N_HEADS, hi, jnp.where(lane < 3 * N_HEADS, mid, lo)).astype(BF16)
    bcum = _dot(ltri, pieces)
    scale = HEAD_DIM ** -0.5
    one_lanes = jnp.where((lane >= 3) & (lane < 6), 1.0, 0.0)

    def lanes012(a, rest):
        p = _split3(a)
        return lambda base: jnp.where(lane == base, p[0], jnp.where(lane == base + 1, p[1],
                                                                      jnp.where(lane == base + 2, p[2], rest)))

    for h in range(N_HEADS):
        ig = gates[:, h:h + 1]
        b = sum(bcum[:, (k + 1) * N_HEADS + h:(k + 1) * N_HEADS + h + 1] for k in range(N_FORGET_COPIES))
        lhs = lanes012(b, one_lanes)(0).astype(BF16)
        rhs = lanes012(ig - b, jnp.where(lane < 3, 1.0, 0.0))(3).astype(BF16)
        dmat = jnp.where(causal, _dot(lhs, rhs, NT_DIMS), NEG_BIG)
        m_prev = m_ref[h][:, 0:1]
        inter = b + m_prev
        m_t = jnp.maximum(jnp.max(dmat, axis=-1, keepdims=True), inter)
        qh = qk[:, h * HEAD_DIM:(h + 1) * HEAD_DIM]
        kh = qk[:, BRANCH_WIDTH + h * HEAD_DIM:BRANCH_WIDTH + (h + 1) * HEAD_DIM] * scale
        vh = v_ref[:, h * HEAD_DIM:(h + 1) * HEAD_DIM]
        qb, kb, vb = qh.astype(BF16), kh.astype(BF16), vh.astype(BF16)
        s = _dot(qb, kb, NT_DIMS) * jnp.exp(dmat - m_t)
        w_inter = jnp.exp(inter - m_t)
        c_prev = c_ref[h]
        n_prev = n_ref[h]
        num = _dot(s.astype(BF16), vb) + w_inter * _dot(qb, c_prev.astype(BF16))
        den = jnp.sum(s, axis=-1, keepdims=True) + w_inter * jnp.sum(qh * n_prev, axis=-1, keepdims=True)
        hm_ref[:, h * HEAD_DIM:(h + 1) * HEAD_DIM] = num / jnp.maximum(jnp.abs(den), jnp.exp(-m_t))
        m_new = m_t[L - 1:L, :]
        b_last = b[L - 1:L, :]
        dec_s = jnp.exp(b_last - b + ig - m_new)
        dec_c = jnp.exp(b_last + m_prev - m_new)
        kd = kh * dec_s
        c_ref[h] = dec_c * c_prev + _dot(kd.astype(BF16), vb, TN_DIMS)
        n_ref[h] = dec_c * n_prev + jnp.sum(kd, axis=0, keepdims=True)
        m_ref[h] = jnp.broadcast_to(m_new, (1, LANE))


def _mlstm(proj, conv0_pad, w_conv, bias_row, c0, n0, m0):
    B, T, _ = proj.shape
    L = min(MLSTM_CHUNK, T)
    wide = lambda blk: pl.BlockSpec((None, L, BRANCH_WIDTH), lambda b, c: (b, c, blk))
    state = lambda shp: pl.BlockSpec((None,) + shp, lambda b, c: (b,) + (0,) * len(shp))
    return pl.pallas_call(
        functools.partial(_mlstm_kernel, L=L),
        grid=(B, T // L),
        in_specs=[
            wide(C_MLQ // 4), wide(C_MLK // 4), wide(C_MLV // 4),
            pl.BlockSpec((None, L, LANE), lambda b, c: (b, c, C_IF)),
            state((HALO, D_MODEL)),
            pl.BlockSpec((CONV_WIDTH, D_MODEL), lambda b, c: (0, 0)),
            pl.BlockSpec((1, LANE), lambda b, c: (0, 0)),
            state((N_HEADS, HEAD_DIM, HEAD_DIM)), state((N_HEADS, 1, HEAD_DIM)), state((N_HEADS, 1, LANE)),
        ],
        out_specs=[
            pl.BlockSpec((None, L, BRANCH_WIDTH), lambda b, c: (b, c, 0)),
            state((N_HEADS, HEAD_DIM, HEAD_DIM)), state((N_HEADS, 1, HEAD_DIM)), state((N_HEADS, 1, LANE)),
        ],
        out_shape=[
            jax.ShapeDtypeStruct((B, T, BRANCH_WIDTH), F32),
            jax.ShapeDtypeStruct((B, N_HEADS, HEAD_DIM, HEAD_DIM), F32),
            jax.ShapeDtypeStruct((B, N_HEADS, 1, HEAD_DIM), F32),
            jax.ShapeDtypeStruct((B, N_HEADS, 1, LANE), F32),
        ],
        scratch_shapes=[pltpu.VMEM((L + HALO, D_MODEL), F32)],
        compiler_params=pltpu.CompilerParams(
            dimension_semantics=("parallel", "arbitrary"), vmem_limit_bytes=VMEM_LIMIT),
        name="mlstm",
    )(proj, proj, proj, proj, conv0_pad, w_conv, bias_row, c0, n0, m0)


def _hgrn_level_operand(lv, bc, g, kk, q_all, L):
    width = bc.shape[-1]
    rowi = lax.broadcasted_iota(jnp.int32, (L, width), 0)
    first_half = (rowi // lv) % 2 == 0
    if lv == 1:
        e = jnp.where(first_half, 0.0, g)
    elif 2 * lv >= SUBLANES:
        b3 = bc.reshape(L // (2 * lv), 2 * lv, width)
        e = (b3 - b3[:, lv - 1:lv, :]).reshape(L, width)
    else:
        b3 = bc.reshape(L // SUBLANES, SUBLANES, width)
        sub = lax.broadcasted_iota(jnp.int32, b3.shape, 1)
        rho = b3[:, lv - 1:lv, :]
        for start in range(2 * lv, SUBLANES, 2 * lv):
            rho = jnp.where(sub >= start, b3[:, start + lv - 1:start + lv, :], rho)
        e = (b3 - rho).reshape(L, width)
    return (jnp.where(first_half, kk, q_all) * jnp.exp(-jnp.abs(e))).astype(BF16)


def _hgrn_kernel(q_ref, f_ref, i_ref, la_ref, l1_ref, oml_ref, lvl_ref, s0_ref, o_ref, st_ref, *, L):
    c = pl.program_id(1)

    @pl.when(c == 0)
    def _():
        st_ref[...] = s0_ref[...]

    f = f_ref[...]
    la = la_ref[...]
    lb_term = l1_ref[...] + _log_sigmoid(f)
    g = jnp.maximum(la, lb_term) + jnp.log1p(jnp.exp(-jnp.abs(la - lb_term)))
    kk = oml_ref[...] / (1.0 + jnp.exp(f))
    q_all = q_ref[...]

    row = lax.broadcasted_iota(jnp.int32, (L, L), 0)
    col = lax.broadcasted_iota(jnp.int32, (L, L), 1)
    ltri = jnp.where(col <= row, 1.0, 0.0).astype(BF16)
    bc = sum(_dot(ltri, piece.astype(BF16)) for piece in _split3(g))
    bc_last = bc[L - 1:L, :]
    qe = (q_all * jnp.exp(bc)).astype(BF16)
    kdec = (kk * jnp.exp(bc_last - bc)).astype(BF16)
    dec_state = jnp.exp(bc_last)
    n_levels = L.bit_length() - 1
    operands = [_hgrn_level_operand(1 << p, bc, g, kk, q_all, L) for p in range(n_levels)]
    lvl = lvl_ref[...]

    for h in range(N_HEADS):
        sl = slice(h * HEAD_DIM, (h + 1) * HEAD_DIM)
        ih = i_ref[:, sl]
        ib = ih.astype(BF16)
        st = st_ref[h]
        a = jnp.zeros((L, L), F32)
        for p, y in enumerate(operands):
            a = jnp.where(lvl == p, _dot(y[:, sl], y[:, sl], NT_DIMS), a)
        diag = jnp.sum(q_all[:, sl] * kk[:, sl], axis=-1, keepdims=True)
        o_ref[:, sl] = _dot(a.astype(BF16), ib) + _dot(qe[:, sl], st.astype(BF16), NT_DIMS) + diag * ih
        st_ref[h] = st * dec_state[:, sl] + _dot(ib, kdec[:, sl], TN_DIMS)


def _hgrn_level_index(L):
    t = np.arange(L)[:, None]
    s = np.arange(L)[None, :]
    x = np.maximum(t ^ s, 1)
    return np.where(s < t, np.floor(np.log2(x)), -1).astype(np.int32)


def _hgrn(proj, la, l1, oml, st0):
    B, T, _ = proj.shape
    L = min(HGRN_CHUNK, T)
    assert L & (L - 1) == 0 and L >= SUBLANES
    wide = lambda blk: pl.BlockSpec((None, L, BRANCH_WIDTH), lambda b, c: (b, c, blk))
    rowspec = pl.BlockSpec((1, BRANCH_WIDTH), lambda b, c: (0, 0))
    state = pl.BlockSpec((None, N_HEADS, HEAD_DIM, HEAD_DIM), lambda b, c: (b, 0, 0, 0))
    return pl.pallas_call(
        functools.partial(_hgrn_kernel, L=L),
        grid=(B, T // L),
        in_specs=[wide(C_HGQ // 4), wide(C_HGF // 4), wide(C_HGI // 4), rowspec, rowspec, rowspec,
                  pl.BlockSpec((L, L), lambda b, c: (0, 0)), state],
        out_specs=[pl.BlockSpec((None, L, BRANCH_WIDTH), lambda b, c: (b, c, 0)), state],
        out_shape=[
            jax.ShapeDtypeStruct((B, T, BRANCH_WIDTH), F32),
            jax.ShapeDtypeStruct((B, N_HEADS, HEAD_DIM, HEAD_DIM), F32),
        ],
        compiler_params=pltpu.CompilerParams(
            dimension_semantics=("parallel", "arbitrary"), vmem_limit_bytes=VMEM_LIMIT),
        name="hgrn2",
    )(proj, proj, proj, la, l1, oml, jnp.asarray(_hgrn_level_index(L)), st0)


SB_KEY_BLOCK = 256
SB_QUERY_BLOCK = 256
SB_ZERO_WEIGHT_LOG = -104.0


def _strict_upper(n):
    r = lax.broadcasted_iota(jnp.int32, (n, n), 0)
    c = lax.broadcasted_iota(jnp.int32, (n, n), 1)
    return jnp.where(r > c, 1.0, 0.0).astype(BF16)


def _sb_block(qb, kb, vb, carry, acc, upper, key_offset=None):
    z = _dot(qb, kb, NT_DIMS)
    sp = jnp.maximum(z, 0.0) + jnp.log1p(jnp.exp(-jnp.abs(z)))
    if key_offset is None:
        mask = None
        lp = -sp
    else:
        r = lax.broadcasted_iota(jnp.int32, z.shape, 0)
        c = lax.broadcasted_iota(jnp.int32, z.shape, 1)
        mask = c + key_offset < r
        lp = jnp.where(mask, -sp, 0.0)
    lp_hi = lp.astype(BF16)
    lp_lo = (lp - lp_hi.astype(F32)).astype(BF16)
    suffix = _dot(lp_hi, upper) + _dot(lp_lo, upper)
    a = jnp.exp((z - sp) + suffix + carry)
    if mask is not None:
        a = jnp.where(mask, a, 0.0)
    acc = acc + _dot(a.astype(BF16), vb)
    carry = carry + jnp.sum(lp, axis=-1, keepdims=True)
    return carry, acc


def _sb_walk(qb, carry, acc, n_blocks, load_block, tk):
    upper = _strict_upper(tk)

    def cond(state):
        t, carry, _ = state
        return jnp.logical_and(t < n_blocks, jnp.max(carry) > SB_ZERO_WEIGHT_LOG)

    def body(state):
        t, carry, acc = state
        kb, vb = load_block(n_blocks - 1 - t)
        carry, acc = _sb_block(qb, kb, vb, carry, acc, upper)
        return t + 1, carry, acc

    _, carry, acc = lax.while_loop(cond, body, (jnp.int32(0), carry, acc))
    return acc


def _sb_prompt_kernel(q_ref, k_ref, v_ref, o_ref, kb_ref, vb_ref, *, tq, tk):
    i = pl.program_id(2)
    T = k_ref.shape[0]
    cast_rows = min(T, 1024)

    @pl.when(i == 0)
    def _():
        def cast(t, _):
            rows = pl.ds(pl.multiple_of(t * cast_rows, cast_rows), cast_rows)
            kb_ref[rows, :] = k_ref[rows, :].astype(BF16)
            vb_ref[rows, :] = v_ref[rows, :].astype(BF16)
            return 0
        lax.fori_loop(0, T // cast_rows, cast, 0)

    qb = (q_ref[...] * (HEAD_DIM ** -0.5)).astype(BF16)
    carry = jnp.zeros((tq, 1), F32)
    acc = jnp.zeros((tq, HEAD_DIM), F32)
    upper = _strict_upper(tk)
    base = i * tq
    for d in reversed(range(tq // tk)):
        rows = pl.ds(pl.multiple_of(base + d * tk, tk), tk)
        carry, acc = _sb_block(qb, kb_ref[rows, :], vb_ref[rows, :], carry, acc, upper, key_offset=d * tk)

    def load_block(j):
        rows = pl.ds(pl.multiple_of(j * tk, tk), tk)
        return kb_ref[rows, :], vb_ref[rows, :]

    o_ref[...] = _sb_walk(qb, carry, acc, i * (tq // tk), load_block, tk)


def _sb_sample_kernel(q_ref, kc_ref, vc_ref, kp_ref, vp_ref, o_ref, *, tk):
    tq = q_ref.shape[0]
    qb = (q_ref[...] * (HEAD_DIM ** -0.5)).astype(BF16)
    carry, acc = _sb_block(qb, kc_ref[...].astype(BF16), vc_ref[...].astype(BF16), jnp.zeros((tq, 1), F32),
                           jnp.zeros((tq, HEAD_DIM), F32), _strict_upper(tq), key_offset=0)

    def load_block(j):
        rows = pl.ds(pl.multiple_of(j * tk, tk), tk)
        return kp_ref[rows, :].astype(BF16), vp_ref[rows, :].astype(BF16)

    o_ref[...] = _sb_walk(qb, carry, acc, kp_ref.shape[0] // tk, load_block, tk)


def _sb_attention(l, proj, k_rows, v_rows, k_past, v_past):
    B, T, _ = proj.shape
    per_head = lambda rows: pl.BlockSpec((None, None, None, rows, HEAD_DIM), lambda b, h, i: (l, b, h, 0, 0))
    if k_past is None:
        tq = min(T, SB_QUERY_BLOCK)
        kernel_fn = functools.partial(_sb_prompt_kernel, tq=tq, tk=min(tq, SB_KEY_BLOCK))
        past_specs, past_args = [], []
        scratch = [pltpu.VMEM((T, HEAD_DIM), BF16), pltpu.VMEM((T, HEAD_DIM), BF16)]
    else:
        tq = T
        P = k_past.shape[3]
        kernel_fn = functools.partial(_sb_sample_kernel, tk=min(P, SB_KEY_BLOCK))
        past_specs, past_args = [per_head(P), per_head(P)], [k_past, v_past]
        scratch = []
    return pl.pallas_call(
        kernel_fn,
        grid=(B, N_HEADS, T // tq),
        in_specs=[pl.BlockSpec((None, tq, HEAD_DIM), lambda b, h, i: (b, i, C_SBQ + h)), per_head(T), per_head(T)]
        + past_specs,
        out_specs=pl.BlockSpec((None, tq, HEAD_DIM), lambda b, h, i: (b, i, h)),
        out_shape=jax.ShapeDtypeStruct((B, T, BRANCH_WIDTH), F32),
        scratch_shapes=scratch,
        compiler_params=pltpu.CompilerParams(
            dimension_semantics=("parallel", "parallel", "arbitrary"), vmem_limit_bytes=VMEM_LIMIT),
        name="stickbreak",
    )(proj, k_rows, v_rows, *past_args)


def _merge_kernel(hm_ref, os_ref, oh_ref, x_ref, p_ref, gpre_ref, wg_ref, gmlh_ref, ghgh_ref, wb_ref, wo_ref,
                  gpost_ref, wpg_ref, wpp_ref, r_ref):
    def head_norm(a):
        parts = []
        for h in range(N_HEADS):
            seg = a[:, h * HEAD_DIM:(h + 1) * HEAD_DIM]
            parts.append(seg * lax.rsqrt(jnp.mean(seg * seg, axis=-1, keepdims=True) + RMS_EPS))
        return jnp.concatenate(parts, axis=-1)

    x = x_ref[...]
    u = _rms_norm_bf16(x, gpre_ref[...])
    proj = lambda start, width: _dot(u, wg_ref[:, start:start + width])
    silu = lambda a: a * _sigmoid(a)
    h_ml = head_norm(hm_ref[...]) * gmlh_ref[...] * _sigmoid(proj(G_MLO, W)) * silu(proj(G_MLZ, W))
    h_sb = os_ref[...] * silu(proj(G_SBZ, W))
    h_hg = head_norm(oh_ref[...]) * ghgh_ref[...] * silu(proj(G_HGZ, W))
    merged = (_sigmoid(proj(G_GATES, D_MODEL)) * _dot(h_ml.astype(BF16), wb_ref[0])
              + _sigmoid(proj(G_GATES + D_MODEL, D_MODEL)) * _dot(h_sb.astype(BF16), wb_ref[1])
              + _sigmoid(proj(G_GATES + 2 * D_MODEL, D_MODEL)) * _dot(h_hg.astype(BF16), wb_ref[2]))
    y = _dot(merged.astype(BF16), wo_ref[...])
    r = x + y * lax.rsqrt(jnp.mean(y * y, axis=-1, keepdims=True) + RMS_EPS) * gpost_ref[...]
    gate = _sigmoid(_dot(r.astype(BF16), wpg_ref[...]))
    r_ref[...] = r + gate * _dot(p_ref[...].astype(BF16), wpp_ref[...])


def _merge(l, hm, o_s, o_h, x2d, p_all, lw):
    n = x2d.shape[0]
    tm = min(n, 256)
    rows = lambda width: pl.BlockSpec((tm, width), lambda i: (i, 0))
    full = lambda a: pl.BlockSpec(a.shape, lambda i: (0,) * a.ndim)
    params = [lw[k] for k in ("g_pre", "w_gate", "g_mlh", "g_hgh", "w_branch", "w_out", "g_post", "w_pg", "w_pp")]
    return pl.pallas_call(
        _merge_kernel,
        grid=(n // tm,),
        in_specs=[rows(W), rows(W), rows(W), rows(D_MODEL), pl.BlockSpec((None, tm, PLE_DIM), lambda i: (l, i, 0))]
        + [full(a) for a in params],
        out_specs=rows(D_MODEL),
        out_shape=jax.ShapeDtypeStruct((n, D_MODEL), F32),
        compiler_params=pltpu.CompilerParams(
            dimension_semantics=("parallel",), vmem_limit_bytes=VMEM_LIMIT),
        name="merge",
    )(hm, o_s, o_h, x2d, p_all, *params)


def _layer(l, depth, x, p_all, c0, n0, m0, conv0, s0, k_past, v_past, kv_prev, lw):
    B, T, _ = x.shape
    proj2d, k_rows, v_rows = _inproj(l, depth, x, lw["g_pre"], lw["w_in"], kv_prev)
    proj = proj2d.reshape(B, T, N_PROJ)

    conv0_pad = jnp.pad(conv0, ((0, 0), (HALO - (CONV_WIDTH - 1), 0), (0, 0)))
    hm, c_new, n_new, m_new = _mlstm(
        proj, conv0_pad, lw["w_conv"], lw["bias_row"], c0, n0.reshape(B, N_HEADS, 1, HEAD_DIM),
        jnp.broadcast_to(m0[:, :, None, None], (B, N_HEADS, 1, LANE)))
    o_s = _sb_attention(l, proj, k_rows, v_rows, k_past, v_past)
    o_h, st_new = _hgrn(proj, lw["log_lb"], lw["log1m_lb"], lw["one_m_lb"], jnp.swapaxes(s0, -1, -2))

    flat = lambda a: a.reshape(B * T, a.shape[-1])
    r = _merge(l, flat(hm), flat(o_s), flat(o_h), flat(x), p_all.reshape(depth, B * T, PLE_DIM), lw)

    conv_new = proj[:, T - (CONV_WIDTH - 1):, C_MLQ * LANE:C_MLQ * LANE + 2 * BRANCH_WIDTH]
    states = (c_new, n_new[:, :, 0, :], m_new[:, :, 0, 0], conv_new, jnp.swapaxes(st_new, -1, -2))
    return r.reshape(B, T, D_MODEL), states, (k_rows, v_rows)


def _layer_weights(l, lower_bounds, g_pre, w_in, b_i, b_f, w_conv, g_mlh, g_hgh, w_branch, w_out, g_post, w_pg, w_pp):
    w = w_in[l].astype(BF16)
    cols = lambda span: w[:, span[0]:span[1]]
    forget = [cols(SRC_ML_F)] * N_FORGET_COPIES
    n_gate = N_HEADS * (1 + N_FORGET_COPIES)
    gate_cols = jnp.pad(jnp.concatenate([cols(SRC_ML_I)] + forget, axis=1), ((0, 0), (0, BRANCH_WIDTH - n_gate)))
    w_re = jnp.concatenate([gate_cols, cols(SRC_ML_QKV), cols(SRC_SB_Q), cols(SRC_HG_QFI), cols(SRC_SB_KV)], axis=1)
    w_gate = jnp.concatenate([cols(SRC_ML_OZ), cols(SRC_SB_Z), cols(SRC_HG_Z_GATES)], axis=1)
    bias_row = jnp.pad(jnp.concatenate([b_i[l]] + [b_f[l]] * N_FORGET_COPIES), (0, LANE - n_gate)).reshape(1, LANE)
    lb = lower_bounds[l].reshape(1, BRANCH_WIDTH)
    return dict(
        g_pre=g_pre[l].reshape(1, D_MODEL), w_in=w_re, w_gate=w_gate, bias_row=bias_row, w_conv=w_conv[l],
        g_mlh=g_mlh[l].reshape(1, BRANCH_WIDTH), g_hgh=g_hgh[l].reshape(1, BRANCH_WIDTH),
        log_lb=jnp.log(lb), log1m_lb=jnp.log1p(-lb), one_m_lb=1.0 - lb,
        w_branch=w_branch[l].astype(BF16), w_out=w_out[l].astype(BF16), g_post=g_post[l].reshape(1, D_MODEL),
        w_pg=w_pg[l].astype(BF16), w_pp=w_pp[l].astype(BF16))


def kernel(x_prompt, x_sample, p_prompt, p_sample, state_mlstm_C, state_mlstm_n, state_mlstm_m, state_mlstm_conv,
           cache_sb_k, cache_sb_v, state_hgrn_S, g_pre, w_in, b_mlstm_i, b_mlstm_f, w_mlstm_conv, g_mlstm_head,
           hgrn_lb_logits, g_hgrn_head, w_branch, w_out, g_post, w_ple_gate, w_ple_proj):
    depth = w_in.shape[0]
    lb_cum = jnp.cumsum(jax.nn.softmax(hgrn_lb_logits.astype(F32), axis=0), axis=0)
    lower_bounds = lb_cum - lb_cum[:1]
    B = x_prompt.shape[0]
    yp, ys = x_prompt, x_sample
    new_p, new_s = [], []
    kv_p = kv_s = None
    for l in range(depth):
        lw = _layer_weights(l, lower_bounds, g_pre, w_in, b_mlstm_i, b_mlstm_f, w_mlstm_conv, g_mlstm_head,
                            g_hgrn_head, w_branch, w_out, g_post, w_ple_gate, w_ple_proj)
        zeros = lambda *shape: jnp.zeros(shape, F32)
        yp, st, kv_p = _layer(l, depth, yp, p_prompt, zeros(B, N_HEADS, HEAD_DIM, HEAD_DIM),
                              zeros(B, N_HEADS, HEAD_DIM), zeros(B, N_HEADS),
                              zeros(B, CONV_WIDTH - 1, 2 * BRANCH_WIDTH), zeros(B, N_HEADS, HEAD_DIM, HEAD_DIM),
                              None, None, kv_p, lw)
        new_p.append(st)
        ys, st, kv_s = _layer(l, depth, ys, p_sample, state_mlstm_C[l], state_mlstm_n[l], state_mlstm_m[l],
                              state_mlstm_conv[l], state_hgrn_S[l], cache_sb_k, cache_sb_v, kv_s, lw)
        new_s.append(st)
    stack = lambda sts, i: jnp.stack([s[i] for s in sts])
    outs = lambda sts, kv: (stack(sts, 0), stack(sts, 1), stack(sts, 2), stack(sts, 3), kv[0], kv[1], stack(sts, 4))
    return (yp, ys, *outs(new_p, kv_p), *outs(new_s, kv_s))
```

```python
import functools

import jax
import jax.numpy as jnp
import numpy as np
from jax import lax
from jax.experimental import pallas as pl
from jax.experimental.pallas import tpu as pltpu

F32 = jnp.float32
BF16 = jnp.bfloat16

D_MODEL = 1024
PLE_DIM = 256
N_HEADS = 4
BRANCH_WIDTH = D_MODEL // 2
HEAD_DIM = BRANCH_WIDTH // N_HEADS
CONV_WIDTH = 4
RMS_EPS = 1e-6
LANE = 128
SUBLANES = 8
HALO = SUBLANES
NEG_BIG = -1e30
LOG2_E = 1.4426950408889634
RECURRENT_CHUNK = 128

W = BRANCH_WIDTH
SRC_ML_QKV = (0, 3 * W)
SRC_ML_OZ = (3 * W, 5 * W)
SRC_ML_I = (5 * W, 5 * W + N_HEADS)
SRC_ML_F = (5 * W + N_HEADS, 5 * W + 2 * N_HEADS)
_S = 5 * W + 2 * N_HEADS
SRC_SB_Q = (_S, _S + W)
SRC_SB_KV = (_S + W, _S + 3 * W)
SRC_SB_Z = (_S + 3 * W, _S + 4 * W)
SRC_HG_QFI = (_S + 4 * W, _S + 7 * W)
SRC_HG_Z_GATES = (_S + 7 * W, _S + 8 * W + 3 * D_MODEL)

C_IF, C_MLQ, C_MLK, C_MLV, C_SBQ, C_HGQ, C_HGF, C_HGI = 0, 4, 8, 12, 16, 20, 24, 28
N_PROJ = 32 * LANE
INPROJ_TILE = 2 * BRANCH_WIDTH
G_MLO, G_MLZ, G_SBZ, G_HGZ, G_GATES = 0, W, 2 * W, 3 * W, 4 * W

VMEM_LIMIT = 56 * 1024 * 1024

NT_DIMS = (((1,), (1,)), ((), ()))
TN_DIMS = (((0,), (0,)), ((), ()))


def _sigmoid(x):
    return 1.0 / (1.0 + jnp.exp(-x))


def _log_sigmoid(x):
    return jnp.minimum(x, 0.0) - jnp.log1p(jnp.exp(-jnp.abs(x)))


def _dot(a, b, dims=None):
    if dims is None:
        dims = (((a.ndim - 1,), (0,)), ((), ()))
    return lax.dot_general(a, b, dims, preferred_element_type=F32)


def _split3(x):
    hi = x.astype(BF16).astype(F32)
    r = x - hi
    mid = r.astype(BF16).astype(F32)
    return hi, mid, r - mid


def _rms_norm_bf16(x, g):
    ms = jnp.mean(x * x, axis=-1, keepdims=True)
    return (x * lax.rsqrt(ms + RMS_EPS) * g).astype(BF16)


def _inproj_kernel(*refs, n_proj_tiles, aliased):
    x_ref, g_ref, w_ref = refs[:3]
    o_ref, k_ref, v_ref, u_ref = refs[-4:]
    j = pl.program_id(1)

    @pl.when(j == 0)
    def _():
        u_ref[...] = _rms_norm_bf16(x_ref[...], g_ref[...])

    res = jnp.dot(u_ref[...], w_ref[...], preferred_element_type=F32)

    @pl.when(j < n_proj_tiles)
    def _():
        o_ref[...] = res

    @pl.when(j == n_proj_tiles)
    def _():
        bt, _, tt, _ = k_ref.shape
        for h in range(N_HEADS):
            k_ref[:, h] = res[:, h * HEAD_DIM:(h + 1) * HEAD_DIM].reshape(bt, tt, HEAD_DIM)
            v_ref[:, h] = res[:, W + h * HEAD_DIM:W + (h + 1) * HEAD_DIM].reshape(bt, tt, HEAD_DIM)


def _inproj(l, depth, x, g_pre, w_bf16, kv_prev):
    B, T, _ = x.shape
    n = B * T
    tm = min(n, 1024)
    bt, tt = max(1, tm // T), min(tm, T)
    n_proj_tiles = N_PROJ // INPROJ_TILE
    kv_block = pl.BlockSpec((None, bt, N_HEADS, tt, HEAD_DIM),
                            lambda i, j: (l, (i * tm) // T // bt, 0, ((i * tm) % T) // tt, 0))
    kv_shape = jax.ShapeDtypeStruct((depth, B, N_HEADS, T, HEAD_DIM), F32)
    aliased = kv_prev is not None
    prev_specs = [pl.BlockSpec(memory_space=pl.ANY)] * 2 if aliased else []
    return pl.pallas_call(
        functools.partial(_inproj_kernel, n_proj_tiles=n_proj_tiles, aliased=aliased),
        grid=(n // tm, n_proj_tiles + 1),
        in_specs=[
            pl.BlockSpec((tm, D_MODEL), lambda i, j: (i, 0)),
            pl.BlockSpec((1, D_MODEL), lambda i, j: (0, 0)),
            pl.BlockSpec((D_MODEL, INPROJ_TILE), lambda i, j: (0, j)),
        ] + prev_specs,
        out_specs=[
            pl.BlockSpec((tm, INPROJ_TILE), lambda i, j: (i, jnp.minimum(j, n_proj_tiles - 1))),
            kv_block, kv_block,
        ],
        out_shape=[jax.ShapeDtypeStruct((n, N_PROJ), F32), kv_shape, kv_shape],
        input_output_aliases={3: 1, 4: 2} if aliased else {},
        scratch_shapes=[pltpu.VMEM((tm, D_MODEL), BF16)],
        compiler_params=pltpu.CompilerParams(
            dimension_semantics=("parallel", "arbitrary"), vmem_limit_bytes=VMEM_LIMIT),
        name="inproj",
    )(x.reshape(n, D_MODEL), g_pre.reshape(1, D_MODEL), w_bf16, *(kv_prev if aliased else ()))


def _mlstm_kernel(q_ref, k_ref, v_ref, gi_ref, conv0_ref, wc_ref, bias_ref, c0_ref, n0_ref, m0_ref,
                  hm_ref, c_ref, n_ref, m_ref, xs_ref, *, L):
    c = pl.program_id(1)

    @pl.when(c == 0)
    def _():
        c_ref[...] = c0_ref[...]
        n_ref[...] = n0_ref[...]
        m_ref[...] = m0_ref[...]
        xs_ref[0:HALO, :] = conv0_ref[...]

    xs_ref[HALO:HALO + L, 0:BRANCH_WIDTH] = q_ref[...]
    xs_ref[HALO:HALO + L, BRANCH_WIDTH:] = k_ref[...]
    base = HALO - (CONV_WIDTH - 1)
    qk = xs_ref[base:base + L, :] * wc_ref[0:1, :]
    for j in range(1, CONV_WIDTH):
        qk = qk + xs_ref[base + j:base + j + L, :] * wc_ref[j:j + 1, :]
    qk = qk * _sigmoid(qk)
    xs_ref[0:HALO, :] = xs_ref[L:L + HALO, :]

    gates = gi_ref[...] + bias_ref[...]
    ig_all = gates[:, :LANE]
    row = lax.broadcasted_iota(jnp.int32, (L, L), 0)
    col = lax.broadcasted_iota(jnp.int32, (L, L), 1)
    causal = col <= row
    ltri = jnp.where(causal, 1.0, 0.0).astype(BF16)
    bcum = sum(_dot(ltri, piece.astype(BF16)) for piece in _split3(_log_sigmoid(gates[:, LANE:])))
    key_term = jnp.transpose(ig_all - bcum)
    scale = HEAD_DIM ** -0.5

    for h in range(N_HEADS):
        ig = ig_all[:, h:h + 1]
        b = bcum[:, h:h + 1]
        dmat = jnp.where(causal, b + key_term[h:h + 1, :], NEG_BIG)
        m_prev = m_ref[h][:, 0:1]
        inter = b + m_prev
        m_t = jnp.maximum(jnp.max(dmat, axis=-1, keepdims=True), inter)
        qh = qk[:, h * HEAD_DIM:(h + 1) * HEAD_DIM]
        kh = qk[:, BRANCH_WIDTH + h * HEAD_DIM:BRANCH_WIDTH + (h + 1) * HEAD_DIM] * scale
        vh = v_ref[:, h * HEAD_DIM:(h + 1) * HEAD_DIM]
        qb, kb, vb = qh.astype(BF16), kh.astype(BF16), vh.astype(BF16)
        s = _dot(qb, kb, NT_DIMS) * jnp.exp(dmat - m_t)
        w_inter = jnp.exp(inter - m_t)
        c_prev = c_ref[h]
        n_prev = n_ref[h]
        num = _dot(s.astype(BF16), vb) + w_inter * _dot(qb, c_prev.astype(BF16))
        den = jnp.sum(s, axis=-1, keepdims=True) + w_inter * jnp.sum(qh * n_prev, axis=-1, keepdims=True)
        hm_ref[:, h * HEAD_DIM:(h + 1) * HEAD_DIM] = num / jnp.maximum(jnp.abs(den), jnp.exp(-m_t))
        m_new = m_t[L - 1:L, :]
        b_last = b[L - 1:L, :]
        dec_s = jnp.exp(b_last - b + ig - m_new)
        dec_c = jnp.exp(b_last + m_prev - m_new)
        kd = kh * dec_s
        c_ref[h] = dec_c * c_prev + _dot(kd.astype(BF16), vb, TN_DIMS)
        n_ref[h] = dec_c * n_prev + jnp.sum(kd, axis=0, keepdims=True)
        m_ref[h] = jnp.broadcast_to(m_new, (1, LANE))


def _mlstm_specs(B, T, L):
    wide = lambda blk: pl.BlockSpec((None, L, BRANCH_WIDTH), lambda b, c: (b, c, blk))
    state = lambda shp: pl.BlockSpec((None,) + shp, lambda b, c: (b,) + (0,) * len(shp))
    in_specs = [
        wide(C_MLQ // 4), wide(C_MLK // 4), wide(C_MLV // 4),
        pl.BlockSpec((None, L, 2 * LANE), lambda b, c: (b, c, C_IF)),
        state((HALO, D_MODEL)),
        pl.BlockSpec((CONV_WIDTH, D_MODEL), lambda b, c: (0, 0)),
        pl.BlockSpec((1, 2 * LANE), lambda b, c: (0, 0)),
        state((N_HEADS, HEAD_DIM, HEAD_DIM)), state((N_HEADS, 1, HEAD_DIM)), state((N_HEADS, 1, LANE)),
    ]
    out_specs = [
        pl.BlockSpec((None, L, BRANCH_WIDTH), lambda b, c: (b, c, 0)),
        state((N_HEADS, HEAD_DIM, HEAD_DIM)), state((N_HEADS, 1, HEAD_DIM)), state((N_HEADS, 1, LANE)),
    ]
    out_shape = [
        jax.ShapeDtypeStruct((B, T, BRANCH_WIDTH), F32),
        jax.ShapeDtypeStruct((B, N_HEADS, HEAD_DIM, HEAD_DIM), F32),
        jax.ShapeDtypeStruct((B, N_HEADS, 1, HEAD_DIM), F32),
        jax.ShapeDtypeStruct((B, N_HEADS, 1, LANE), F32),
    ]
    return in_specs, out_specs, out_shape


def _hgrn_level_operand(lv, bc, g, kk, q_all, L):
    width = bc.shape[-1]
    rowi = lax.broadcasted_iota(jnp.int32, (L, width), 0)
    first_half = (rowi // lv) % 2 == 0
    if lv == 1:
        e = jnp.where(first_half, 0.0, g)
    elif 2 * lv >= SUBLANES:
        b3 = bc.reshape(L // (2 * lv), 2 * lv, width)
        e = (b3 - b3[:, lv - 1:lv, :]).reshape(L, width)
    else:
        b3 = bc.reshape(L // SUBLANES, SUBLANES, width)
        sub = lax.broadcasted_iota(jnp.int32, b3.shape, 1)
        rho = b3[:, lv - 1:lv, :]
        for start in range(2 * lv, SUBLANES, 2 * lv):
            rho = jnp.where(sub >= start, b3[:, start + lv - 1:start + lv, :], rho)
        e = (b3 - rho).reshape(L, width)
    return (jnp.where(first_half, kk, q_all) * jnp.exp2(-jnp.abs(e))).astype(BF16)


def _hgrn_kernel(q_ref, f_ref, i_ref, la_ref, l1_ref, oml_ref, lvl_ref, s0_ref, o_ref, st_ref, *, L):
    c = pl.program_id(1)

    @pl.when(c == 0)
    def _():
        st_ref[...] = s0_ref[...]

    f = f_ref[...]
    la = la_ref[...]
    lb_term = l1_ref[...] + _log_sigmoid(f)
    g = jnp.maximum(la, lb_term) + jnp.log1p(jnp.exp(-jnp.abs(la - lb_term)))
    kk = oml_ref[...] / (1.0 + jnp.exp(f))
    q_all = q_ref[...]

    row = lax.broadcasted_iota(jnp.int32, (L, L), 0)
    col = lax.broadcasted_iota(jnp.int32, (L, L), 1)
    ltri = jnp.where(col <= row, 1.0, 0.0).astype(BF16)
    g = g * LOG2_E
    bc = sum(_dot(ltri, piece.astype(BF16)) for piece in _split3(g))
    bc_last = bc[L - 1:L, :]
    qe = (q_all * jnp.exp2(bc)).astype(BF16)
    kdec = (kk * jnp.exp2(bc_last - bc)).astype(BF16)
    dec_state = jnp.exp2(bc_last)
    n_levels = L.bit_length() - 1
    operands = [_hgrn_level_operand(1 << p, bc, g, kk, q_all, L) for p in range(n_levels)]
    lvl = lvl_ref[...]

    for h in range(N_HEADS):
        sl = slice(h * HEAD_DIM, (h + 1) * HEAD_DIM)
        ih = i_ref[:, sl]
        ib = ih.astype(BF16)
        st = st_ref[h]
        a = jnp.zeros((L, L), F32)
        for p, y in enumerate(operands):
            a = jnp.where(lvl == p, _dot(y[:, sl], y[:, sl], NT_DIMS), a)
        diag = jnp.sum(q_all[:, sl] * kk[:, sl], axis=-1, keepdims=True)
        o_ref[:, sl] = _dot(a.astype(BF16), ib) + _dot(qe[:, sl], st.astype(BF16), NT_DIMS) + diag * ih
        st_ref[h] = st * dec_state[:, sl] + _dot(ib, kdec[:, sl], TN_DIMS)


def _hgrn_level_index(L):
    t = np.arange(L)[:, None]
    s = np.arange(L)[None, :]
    x = np.maximum(t ^ s, 1)
    return np.where(s < t, np.floor(np.log2(x)), -1).astype(np.int32)


def _hgrn_specs(B, T, L):
    assert L & (L - 1) == 0 and L >= SUBLANES
    wide = lambda blk: pl.BlockSpec((None, L, BRANCH_WIDTH), lambda b, c: (b, c, blk))
    rowspec = pl.BlockSpec((1, BRANCH_WIDTH), lambda b, c: (0, 0))
    state = pl.BlockSpec((None, N_HEADS, HEAD_DIM, HEAD_DIM), lambda b, c: (b, 0, 0, 0))
    in_specs = [wide(C_HGQ // 4), wide(C_HGF // 4), wide(C_HGI // 4), rowspec, rowspec, rowspec,
                pl.BlockSpec((L, L), lambda b, c: (0, 0)), state]
    out_specs = [pl.BlockSpec((None, L, BRANCH_WIDTH), lambda b, c: (b, c, 0)), state]
    out_shape = [
        jax.ShapeDtypeStruct((B, T, BRANCH_WIDTH), F32),
        jax.ShapeDtypeStruct((B, N_HEADS, HEAD_DIM, HEAD_DIM), F32),
    ]
    return in_specs, out_specs, out_shape


def _recurrent_kernel(*refs, L, n_in, n_out):
    m_in, h_in = refs[:n_in[0]], refs[n_in[0]:sum(n_in)]
    outs = refs[sum(n_in):sum(n_in) + sum(n_out)]
    m_out, h_out = outs[:n_out[0]], outs[n_out[0]:]
    xs_ref = refs[-1]
    _mlstm_kernel(*m_in, *m_out, xs_ref, L=L)
    _hgrn_kernel(*h_in, *h_out, L=L)


def _recurrent(proj, conv0_pad, w_conv, bias_row, c0, n0, m0, la, l1, oml, st0):
    B, T, _ = proj.shape
    L = min(RECURRENT_CHUNK, T)
    m_specs, h_specs = _mlstm_specs(B, T, L), _hgrn_specs(B, T, L)
    n_in, n_out = (len(m_specs[0]), len(h_specs[0])), (len(m_specs[1]), len(h_specs[1]))
    return pl.pallas_call(
        functools.partial(_recurrent_kernel, L=L, n_in=n_in, n_out=n_out),
        grid=(B, T // L),
        in_specs=m_specs[0] + h_specs[0],
        out_specs=m_specs[1] + h_specs[1],
        out_shape=m_specs[2] + h_specs[2],
        scratch_shapes=[pltpu.VMEM((L + HALO, D_MODEL), F32)],
        compiler_params=pltpu.CompilerParams(
            dimension_semantics=("parallel", "arbitrary"), vmem_limit_bytes=VMEM_LIMIT),
        name="recurrent",
    )(proj, proj, proj, proj, conv0_pad, w_conv, bias_row, c0, n0, m0,
      proj, proj, proj, la, l1, oml, jnp.asarray(_hgrn_level_index(L)), st0)


SB_KEY_BLOCK = 256
SB_QUERY_BLOCK = 512
SB_ZERO_WEIGHT_LOG = -104.0


def _strict_upper(n):
    r = lax.broadcasted_iota(jnp.int32, (n, n), 0)
    c = lax.broadcasted_iota(jnp.int32, (n, n), 1)
    return jnp.where(r > c, 1.0, 0.0).astype(BF16)


def _sb_block(qb, kb, vb, carry, acc, upper, key_offset=None, valid=None):
    z = _dot(qb, kb, NT_DIMS)
    sp = jnp.maximum(z, 0.0) + jnp.log1p(jnp.exp(-jnp.abs(z)))
    if valid is not None:
        mask = valid
        lp = jnp.where(mask, -sp, 0.0)
    elif key_offset is None:
        mask = None
        lp = -sp
    else:
        r = lax.broadcasted_iota(jnp.int32, z.shape, 0)
        c = lax.broadcasted_iota(jnp.int32, z.shape, 1)
        mask = c + key_offset < r
        lp = jnp.where(mask, -sp, 0.0)
    lp_hi = lp.astype(BF16)
    lp_lo = (lp - lp_hi.astype(F32)).astype(BF16)
    suffix = _dot(lp_hi, upper) + _dot(lp_lo, upper)
    a = jnp.exp((z - sp) + suffix + carry)
    if mask is not None:
        a = jnp.where(mask, a, 0.0)
    acc = acc + _dot(a.astype(BF16), vb)
    carry = carry + jnp.sum(lp, axis=-1, keepdims=True)
    return carry, acc


def _sb_walk(qb, carry, acc, n_blocks, load_block, tk):
    upper = _strict_upper(tk)

    def cond(state):
        t, carry, _ = state
        return jnp.logical_and(t < n_blocks, jnp.max(carry) > SB_ZERO_WEIGHT_LOG)

    def body(state):
        t, carry, acc = state
        kb, vb = load_block(n_blocks - 1 - t)
        carry, acc = _sb_block(qb, kb, vb, carry, acc, upper)
        return t + 1, carry, acc

    _, carry, acc = lax.while_loop(cond, body, (jnp.int32(0), carry, acc))
    return acc


def _sb_prompt_kernel(q_ref, k_ref, v_ref, o_ref, kb_ref, vb_ref, *, tq, tk):
    i = pl.program_id(2)
    T = k_ref.shape[0]
    cast_rows = min(T, 1024)

    @pl.when(i == 0)
    def _():
        def cast(t, _):
            rows = pl.ds(pl.multiple_of(t * cast_rows, cast_rows), cast_rows)
            kb_ref[rows, :] = k_ref[rows, :].astype(BF16)
            vb_ref[rows, :] = v_ref[rows, :].astype(BF16)
            return 0
        lax.fori_loop(0, T // cast_rows, cast, 0)

    def load_block(j):
        rows = pl.ds(pl.multiple_of(j * tk, tk), tk)
        return kb_ref[rows, :], vb_ref[rows, :]

    upper = _strict_upper(tk)
    groups = []
    for s in range(tq // tk):
        blk = i * (tq // tk) + s
        qb = (q_ref[s * tk:(s + 1) * tk, :] * (HEAD_DIM ** -0.5)).astype(BF16)
        carry, acc = _sb_block(qb, *load_block(blk), jnp.zeros((tk, 1), F32), jnp.zeros((tk, HEAD_DIM), F32), upper,
                               key_offset=0)
        carry, acc = _sb_block(qb, *load_block(jnp.maximum(blk - 1, 0)), carry, acc, upper, valid=blk >= 1)
        groups.append((qb, carry, acc, jnp.maximum(blk - 1, 0)))
    for s, (qb, carry, acc, n_before) in enumerate(groups):
        o_ref[s * tk:(s + 1) * tk, :] = _sb_walk(qb, carry, acc, n_before, load_block, tk)


def _sb_sample_kernel(q_ref, kc_ref, vc_ref, kp_ref, vp_ref, o_ref, *, tk):
    tq = q_ref.shape[0]
    qb = (q_ref[...] * (HEAD_DIM ** -0.5)).astype(BF16)
    carry, acc = _sb_block(qb, kc_ref[...].astype(BF16), vc_ref[...].astype(BF16), jnp.zeros((tq, 1), F32),
                           jnp.zeros((tq, HEAD_DIM), F32), _strict_upper(tq), key_offset=0)

    def load_block(j):
        rows = pl.ds(pl.multiple_of(j * tk, tk), tk)
        return kp_ref[rows, :].astype(BF16), vp_ref[rows, :].astype(BF16)

    o_ref[...] = _sb_walk(qb, carry, acc, kp_ref.shape[0] // tk, load_block, tk)


def _sb_attention(l, proj, k_rows, v_rows, k_past, v_past):
    B, T, _ = proj.shape
    per_head = lambda rows: pl.BlockSpec((None, None, None, rows, HEAD_DIM), lambda b, h, i: (l, b, h, 0, 0))
    if k_past is None:
        tq = min(T, SB_QUERY_BLOCK)
        kernel_fn = functools.partial(_sb_prompt_kernel, tq=tq, tk=min(tq, SB_KEY_BLOCK))
        past_specs, past_args = [], []
        scratch = [pltpu.VMEM((T, HEAD_DIM), BF16), pltpu.VMEM((T, HEAD_DIM), BF16)]
    else:
        tq = T
        P = k_past.shape[3]
        kernel_fn = functools.partial(_sb_sample_kernel, tk=min(P, SB_KEY_BLOCK))
        past_specs, past_args = [per_head(P), per_head(P)], [k_past, v_past]
        scratch = []
    return pl.pallas_call(
        kernel_fn,
        grid=(B, N_HEADS, T // tq),
        in_specs=[pl.BlockSpec((None, tq, HEAD_DIM), lambda b, h, i: (b, i, C_SBQ + h)), per_head(T), per_head(T)]
        + past_specs,
        out_specs=pl.BlockSpec((None, tq, HEAD_DIM), lambda b, h, i: (b, i, h)),
        out_shape=jax.ShapeDtypeStruct((B, T, BRANCH_WIDTH), F32),
        scratch_shapes=scratch,
        compiler_params=pltpu.CompilerParams(
            dimension_semantics=("parallel", "parallel", "arbitrary"), vmem_limit_bytes=VMEM_LIMIT),
        name="stickbreak",
    )(proj, k_rows, v_rows, *past_args)


def _merge_kernel(hm_ref, os_ref, oh_ref, x_ref, p_ref, gpre_ref, wg_ref, gmlh_ref, ghgh_ref, wb_ref, wo_ref,
                  gpost_ref, wpg_ref, wpp_ref, r_ref):
    def head_norm(a):
        parts = []
        for h in range(N_HEADS):
            seg = a[:, h * HEAD_DIM:(h + 1) * HEAD_DIM]
            parts.append(seg * lax.rsqrt(jnp.mean(seg * seg, axis=-1, keepdims=True) + RMS_EPS))
        return jnp.concatenate(parts, axis=-1)

    x = x_ref[...]
    u = _rms_norm_bf16(x, gpre_ref[...])
    proj = lambda start, width: _dot(u, wg_ref[:, start:start + width])
    silu = lambda a: a * _sigmoid(a)
    h_ml = head_norm(hm_ref[...]) * gmlh_ref[...] * _sigmoid(proj(G_MLO, W)) * silu(proj(G_MLZ, W))
    h_sb = os_ref[...] * silu(proj(G_SBZ, W))
    h_hg = head_norm(oh_ref[...]) * ghgh_ref[...] * silu(proj(G_HGZ, W))
    merged = (_sigmoid(proj(G_GATES, D_MODEL)) * _dot(h_ml.astype(BF16), wb_ref[0])
              + _sigmoid(proj(G_GATES + D_MODEL, D_MODEL)) * _dot(h_sb.astype(BF16), wb_ref[1])
              + _sigmoid(proj(G_GATES + 2 * D_MODEL, D_MODEL)) * _dot(h_hg.astype(BF16), wb_ref[2]))
    y = _dot(merged.astype(BF16), wo_ref[...])
    r = x + y * lax.rsqrt(jnp.mean(y * y, axis=-1, keepdims=True) + RMS_EPS) * gpost_ref[...]
    gate = _sigmoid(_dot(r.astype(BF16), wpg_ref[...]))
    r_ref[...] = r + gate * _dot(p_ref[...].astype(BF16), wpp_ref[...])


def _merge(l, hm, o_s, o_h, x2d, p_all, lw):
    n = x2d.shape[0]
    tm = min(n, 256)
    rows = lambda width: pl.BlockSpec((tm, width), lambda i: (i, 0))
    full = lambda a: pl.BlockSpec(a.shape, lambda i: (0,) * a.ndim)
    params = [lw[k] for k in ("g_pre", "w_gate", "g_mlh", "g_hgh", "w_branch", "w_out", "g_post", "w_pg", "w_pp")]
    return pl.pallas_call(
        _merge_kernel,
        grid=(n // tm,),
        in_specs=[rows(W), rows(W), rows(W), rows(D_MODEL), pl.BlockSpec((None, tm, PLE_DIM), lambda i: (l, i, 0))]
        + [full(a) for a in params],
        out_specs=rows(D_MODEL),
        out_shape=jax.ShapeDtypeStruct((n, D_MODEL), F32),
        compiler_params=pltpu.CompilerParams(
            dimension_semantics=("parallel",), vmem_limit_bytes=VMEM_LIMIT),
        name="merge",
    )(hm, o_s, o_h, x2d, p_all, *params)


def _layer(l, depth, x, p_all, c0, n0, m0, conv0, s0, k_past, v_past, kv_prev, lw):
    B, T, _ = x.shape
    proj2d, k_rows, v_rows = _inproj(l, depth, x, lw["g_pre"], lw["w_in"], kv_prev)
    proj = proj2d.reshape(B, T, N_PROJ)

    conv0_pad = jnp.pad(conv0, ((0, 0), (HALO - (CONV_WIDTH - 1), 0), (0, 0)))
    hm, c_new, n_new, m_new, o_h, st_new = _recurrent(
        proj, conv0_pad, lw["w_conv"], lw["bias_row"], c0, n0.reshape(B, N_HEADS, 1, HEAD_DIM),
        jnp.broadcast_to(m0[:, :, None, None], (B, N_HEADS, 1, LANE)),
        lw["log_lb"], lw["log1m_lb"], lw["one_m_lb"], jnp.swapaxes(s0, -1, -2))
    o_s = _sb_attention(l, proj, k_rows, v_rows, k_past, v_past)

    flat = lambda a: a.reshape(B * T, a.shape[-1])
    r = _merge(l, flat(hm), flat(o_s), flat(o_h), flat(x), p_all.reshape(depth, B * T, PLE_DIM), lw)

    conv_new = proj[:, T - (CONV_WIDTH - 1):, C_MLQ * LANE:C_MLQ * LANE + 2 * BRANCH_WIDTH]
    states = (c_new, n_new[:, :, 0, :], m_new[:, :, 0, 0], conv_new, jnp.swapaxes(st_new, -1, -2))
    return r.reshape(B, T, D_MODEL), states, (k_rows, v_rows)


def _layer_weights(l, lower_bounds, g_pre, w_in, b_i, b_f, w_conv, g_mlh, g_hgh, w_branch, w_out, g_post, w_pg, w_pp):
    w = w_in[l].astype(BF16)
    cols = lambda span: w[:, span[0]:span[1]]
    lane_pad = lambda a, width: jnp.pad(a, ((0, 0), (0, width - a.shape[1])))
    gate_cols = jnp.concatenate(
        [lane_pad(cols(SRC_ML_I), LANE), lane_pad(cols(SRC_ML_F), BRANCH_WIDTH - LANE)], axis=1)
    w_re = jnp.concatenate([gate_cols, cols(SRC_ML_QKV), cols(SRC_SB_Q), cols(SRC_HG_QFI), cols(SRC_SB_KV)], axis=1)
    w_gate = jnp.concatenate([cols(SRC_ML_OZ), cols(SRC_SB_Z), cols(SRC_HG_Z_GATES)], axis=1)
    bias_row = jnp.concatenate([jnp.pad(b_i[l], (0, LANE - N_HEADS)), jnp.pad(b_f[l], (0, LANE - N_HEADS))])
    bias_row = bias_row.reshape(1, 2 * LANE)
    lb = lower_bounds[l].reshape(1, BRANCH_WIDTH)
    return dict(
        g_pre=g_pre[l].reshape(1, D_MODEL), w_in=w_re, w_gate=w_gate, bias_row=bias_row, w_conv=w_conv[l],
        g_mlh=g_mlh[l].reshape(1, BRANCH_WIDTH), g_hgh=g_hgh[l].reshape(1, BRANCH_WIDTH),
        log_lb=jnp.log(lb), log1m_lb=jnp.log1p(-lb), one_m_lb=1.0 - lb,
        w_branch=w_branch[l].astype(BF16), w_out=w_out[l].astype(BF16), g_post=g_post[l].reshape(1, D_MODEL),
        w_pg=w_pg[l].astype(BF16), w_pp=w_pp[l].astype(BF16))


def kernel(x_prompt, x_sample, p_prompt, p_sample, state_mlstm_C, state_mlstm_n, state_mlstm_m, state_mlstm_conv,
           cache_sb_k, cache_sb_v, state_hgrn_S, g_pre, w_in, b_mlstm_i, b_mlstm_f, w_mlstm_conv, g_mlstm_head,
           hgrn_lb_logits, g_hgrn_head, w_branch, w_out, g_post, w_ple_gate, w_ple_proj):
    depth = w_in.shape[0]
    lb_cum = jnp.cumsum(jax.nn.softmax(hgrn_lb_logits.astype(F32), axis=0), axis=0)
    lower_bounds = lb_cum - lb_cum[:1]
    B = x_prompt.shape[0]
    yp, ys = x_prompt, x_sample
    new_p, new_s = [], []
    kv_p = kv_s = None
    for l in range(depth):
        lw = _layer_weights(l, lower_bounds, g_pre, w_in, b_mlstm_i, b_mlstm_f, w_mlstm_conv, g_mlstm_head,
                            g_hgrn_head, w_branch, w_out, g_post, w_ple_gate, w_ple_proj)
        zeros = lambda *shape: jnp.zeros(shape, F32)
        yp, st, kv_p = _layer(l, depth, yp, p_prompt, zeros(B, N_HEADS, HEAD_DIM, HEAD_DIM),
                              zeros(B, N_HEADS, HEAD_DIM), zeros(B, N_HEADS),
                              zeros(B, CONV_WIDTH - 1, 2 * BRANCH_WIDTH), zeros(B, N_HEADS, HEAD_DIM, HEAD_DIM),
                              None, None, kv_p, lw)
        new_p.append(st)
        ys, st, kv_s = _layer(l, depth, ys, p_sample, state_mlstm_C[l], state_mlstm_n[l], state_mlstm_m[l],
                              state_mlstm_conv[l], state_hgrn_S[l], cache_sb_k, cache_sb_v, kv_s, lw)
        new_s.append(st)
    stack = lambda sts, i: jnp.stack([s[i] for s in sts])
    outs = lambda sts, kv: (stack(sts, 0), stack(sts, 1), stack(sts, 2), stack(sts, 3), kv[0], kv[1], stack(sts, 4))
    return (yp, ys, *outs(new_p, kv_p), *outs(new_s, kv_s))
```

```python
import functools

import jax
import jax.numpy as jnp
import numpy as np
from jax import lax
from jax.experimental import pallas as pl
from jax.experimental.pallas import tpu as pltpu

F32 = jnp.float32
BF16 = jnp.bfloat16

D_MODEL = 1024
PLE_DIM = 256
N_HEADS = 4
BRANCH_WIDTH = D_MODEL // 2
HEAD_DIM = BRANCH_WIDTH // N_HEADS
CONV_WIDTH = 4
RMS_EPS = 1e-6
LANE = 128
SUBLANES = 8
HALO = SUBLANES
NEG_BIG = -1e30
LOG2_E = 1.4426950408889634
RECURRENT_CHUNK = 128

W = BRANCH_WIDTH
SRC_ML_QKV = (0, 3 * W)
SRC_ML_OZ = (3 * W, 5 * W)
SRC_ML_I = (5 * W, 5 * W + N_HEADS)
SRC_ML_F = (5 * W + N_HEADS, 5 * W + 2 * N_HEADS)
_S = 5 * W + 2 * N_HEADS
SRC_SB_QKV = (_S, _S + 3 * W)
SRC_SB_Z = (_S + 3 * W, _S + 4 * W)
SRC_HG_QFI = (_S + 4 * W, _S + 7 * W)
SRC_HG_Z_GATES = (_S + 7 * W, _S + 8 * W + 3 * D_MODEL)

INPROJ_TILE = 3 * BRANCH_WIDTH
INPROJ_SOURCES = (SRC_ML_QKV, SRC_SB_QKV, SRC_HG_QFI)
SB_TILE = 1
N_PROJ = len(INPROJ_SOURCES) * INPROJ_TILE
C_MLQ, C_MLK, C_MLV, C_SBQ, C_HGQ, C_HGF, C_HGI = 0, 4, 8, 12, 24, 28, 32
MERGE_SOURCES = (SRC_ML_OZ, SRC_SB_Z, SRC_HG_Z_GATES)
G_MLO, G_MLZ, G_SBZ, G_HGZ, G_GATES = (0, 0), (0, W), (1, 0), (2, 0), (2, W)

VMEM_LIMIT = 56 * 1024 * 1024

NT_DIMS = (((1,), (1,)), ((), ()))
TN_DIMS = (((0,), (0,)), ((), ()))


def _sigmoid(x):
    return 1.0 / (1.0 + jnp.exp(-x))


def _softplus_neg_abs(x):
    return jnp.log(1.0 + jnp.exp(-jnp.abs(x)))


def _log_sigmoid(x):
    return jnp.minimum(x, 0.0) - _softplus_neg_abs(x)


def _dot(a, b, dims=None):
    if dims is None:
        dims = (((a.ndim - 1,), (0,)), ((), ()))
    return lax.dot_general(a, b, dims, preferred_element_type=F32)


def _split3(x):
    hi = x.astype(BF16).astype(F32)
    r = x - hi
    mid = r.astype(BF16).astype(F32)
    return hi, mid, r - mid


def _rms_norm_bf16(x, g):
    ms = jnp.mean(x * x, axis=-1, keepdims=True)
    return (x * lax.rsqrt(ms + RMS_EPS) * g).astype(BF16)


def _inproj_kernel(*refs, aliased):
    x_ref, g_ref, w_ref, wif_ref = refs[:4]
    o_ref, gi_ref, k_ref, v_ref, u_ref = refs[-5:]
    j = pl.program_id(1)

    @pl.when(j == 0)
    def _():
        u_ref[...] = _rms_norm_bf16(x_ref[...], g_ref[...])
        gi_ref[...] = jnp.dot(u_ref[...], wif_ref[...], preferred_element_type=F32)

    res = jnp.dot(u_ref[...], w_ref[...], preferred_element_type=F32)
    o_ref[...] = res

    @pl.when(j == SB_TILE)
    def _():
        if aliased:
            k_dst, v_dst = k_ref, v_ref
        else:
            k_dst, v_dst = k_ref.at[0], v_ref.at[0]
            if k_ref.shape[0] > 1:
                k_ref[1:] = jnp.zeros((k_ref.shape[0] - 1,) + k_ref.shape[1:], F32)
                v_ref[1:] = jnp.zeros((v_ref.shape[0] - 1,) + v_ref.shape[1:], F32)
        bt, _, tt, _ = k_dst.shape
        for h in range(N_HEADS):
            k_dst[:, h] = res[:, W + h * HEAD_DIM:W + (h + 1) * HEAD_DIM].reshape(bt, tt, HEAD_DIM)
            v_dst[:, h] = res[:, 2 * W + h * HEAD_DIM:2 * W + (h + 1) * HEAD_DIM].reshape(bt, tt, HEAD_DIM)


def _inproj(l, depth, x, g_pre, w_tiles, w_if, kv_prev):
    B, T, _ = x.shape
    n = B * T
    tm = min(n, 1024)
    bt, tt = max(1, tm // T), min(tm, T)
    aliased = kv_prev is not None
    assert aliased == (l > 0)
    kv_index = lambda i, j: ((i * tm) // T // bt, 0, ((i * tm) % T) // tt, 0)
    if aliased:
        kv_block = pl.BlockSpec((None, bt, N_HEADS, tt, HEAD_DIM), lambda i, j: (l,) + kv_index(i, j))
    else:
        kv_block = pl.BlockSpec((depth, bt, N_HEADS, tt, HEAD_DIM), lambda i, j: (0,) + kv_index(i, j))
    kv_shape = jax.ShapeDtypeStruct((depth, B, N_HEADS, T, HEAD_DIM), F32)
    prev_specs = [pl.BlockSpec(memory_space=pl.ANY)] * 2 if aliased else []
    return pl.pallas_call(
        functools.partial(_inproj_kernel, aliased=aliased),
        grid=(n // tm, len(INPROJ_SOURCES)),
        in_specs=[
            pl.BlockSpec((tm, D_MODEL), lambda i, j: (i, 0)),
            pl.BlockSpec((1, D_MODEL), lambda i, j: (0, 0)),
            pl.BlockSpec((None, D_MODEL, INPROJ_TILE), lambda i, j: (j, 0, 0)),
            pl.BlockSpec((D_MODEL, 2 * LANE), lambda i, j: (0, 0)),
        ] + prev_specs,
        out_specs=[
            pl.BlockSpec((tm, INPROJ_TILE), lambda i, j: (i, j)),
            pl.BlockSpec((tm, 2 * LANE), lambda i, j: (i, 0)),
            kv_block, kv_block,
        ],
        out_shape=[jax.ShapeDtypeStruct((n, N_PROJ), F32), jax.ShapeDtypeStruct((n, 2 * LANE), F32),
                   kv_shape, kv_shape],
        input_output_aliases={4: 2, 5: 3} if aliased else {},
        scratch_shapes=[pltpu.VMEM((tm, D_MODEL), BF16)],
        compiler_params=pltpu.CompilerParams(
            dimension_semantics=("parallel", "arbitrary"), vmem_limit_bytes=VMEM_LIMIT),
        name="inproj",
    )(x.reshape(n, D_MODEL), g_pre.reshape(1, D_MODEL), w_tiles, w_if, *(kv_prev if aliased else ()))


def _mlstm_kernel(q_ref, k_ref, v_ref, gi_ref, conv0_ref, wc_ref, bias_ref, c0_ref, n0_ref, m0_ref,
                  hm_ref, c_ref, n_ref, m_ref, xs_ref, *, L):
    c = pl.program_id(1)

    @pl.when(c == 0)
    def _():
        c_ref[...] = c0_ref[...]
        n_ref[...] = n0_ref[...]
        m_ref[...] = m0_ref[...]
        xs_ref[0:HALO, :] = conv0_ref[...]

    xs_ref[HALO:HALO + L, 0:BRANCH_WIDTH] = q_ref[...]
    xs_ref[HALO:HALO + L, BRANCH_WIDTH:] = k_ref[...]
    base = HALO - (CONV_WIDTH - 1)
    qk = xs_ref[base:base + L, :] * wc_ref[0:1, :]
    for j in range(1, CONV_WIDTH):
        qk = qk + xs_ref[base + j:base + j + L, :] * wc_ref[j:j + 1, :]
    qk = qk * _sigmoid(qk)
    xs_ref[0:HALO, :] = xs_ref[L:L + HALO, :]

    gates = gi_ref[...] + bias_ref[...]
    ig_all = gates[:, :LANE]
    row = lax.broadcasted_iota(jnp.int32, (L, L), 0)
    col = lax.broadcasted_iota(jnp.int32, (L, L), 1)
    causal = col <= row
    ltri = jnp.where(causal, 1.0, 0.0).astype(BF16)
    bcum = sum(_dot(ltri, piece.astype(BF16)) for piece in _split3(_log_sigmoid(gates[:, LANE:])))
    key_term = jnp.transpose(ig_all - bcum)
    scale = HEAD_DIM ** -0.5

    for h in range(N_HEADS):
        ig = ig_all[:, h:h + 1]
        b = bcum[:, h:h + 1]
        dmat = jnp.where(causal, b + key_term[h:h + 1, :], NEG_BIG)
        m_prev = m_ref[h][:, 0:1]
        inter = b + m_prev
        m_t = jnp.maximum(jnp.max(dmat, axis=-1, keepdims=True), inter)
        qh = qk[:, h * HEAD_DIM:(h + 1) * HEAD_DIM]
        kh = qk[:, BRANCH_WIDTH + h * HEAD_DIM:BRANCH_WIDTH + (h + 1) * HEAD_DIM] * scale
        vh = v_ref[:, h * HEAD_DIM:(h + 1) * HEAD_DIM]
        qb, kb, vb = qh.astype(BF16), kh.astype(BF16), vh.astype(BF16)
        s = _dot(qb, kb, NT_DIMS) * jnp.exp(dmat - m_t)
        w_inter = jnp.exp(inter - m_t)
        c_prev = c_ref[h]
        n_prev = n_ref[h]
        num = _dot(s.astype(BF16), vb) + w_inter * _dot(qb, c_prev.astype(BF16))
        den = jnp.sum(s, axis=-1, keepdims=True) + w_inter * jnp.sum(qh * n_prev, axis=-1, keepdims=True)
        hm_ref[:, h * HEAD_DIM:(h + 1) * HEAD_DIM] = num / jnp.maximum(jnp.abs(den), jnp.exp(-m_t))
        m_new = m_t[L - 1:L, :]
        b_last = b[L - 1:L, :]
        dec_s = jnp.exp(b_last - b + ig - m_new)
        dec_c = jnp.exp(b_last + m_prev - m_new)
        kd = kh * dec_s
        c_ref[h] = dec_c * c_prev + _dot(kd.astype(BF16), vb, TN_DIMS)
        n_ref[h] = dec_c * n_prev + jnp.sum(kd, axis=0, keepdims=True)
        m_ref[h] = jnp.broadcast_to(m_new, (1, LANE))


def _mlstm_specs(B, T, L):
    wide = lambda blk: pl.BlockSpec((None, L, BRANCH_WIDTH), lambda b, c: (b, c, blk))
    state = lambda shp: pl.BlockSpec((None,) + shp, lambda b, c: (b,) + (0,) * len(shp))
    in_specs = [
        wide(C_MLQ // 4), wide(C_MLK // 4), wide(C_MLV // 4),
        pl.BlockSpec((None, L, 2 * LANE), lambda b, c: (b, c, 0)),
        state((HALO, D_MODEL)),
        pl.BlockSpec((CONV_WIDTH, D_MODEL), lambda b, c: (0, 0)),
        pl.BlockSpec((1, 2 * LANE), lambda b, c: (0, 0)),
        state((N_HEADS, HEAD_DIM, HEAD_DIM)), state((N_HEADS, 1, HEAD_DIM)), state((N_HEADS, 1, LANE)),
    ]
    out_specs = [
        pl.BlockSpec((None, L, BRANCH_WIDTH), lambda b, c: (b, c, 0)),
        state((N_HEADS, HEAD_DIM, HEAD_DIM)), state((N_HEADS, 1, HEAD_DIM)), state((N_HEADS, 1, LANE)),
    ]
    out_shape = [
        jax.ShapeDtypeStruct((B, T, BRANCH_WIDTH), F32),
        jax.ShapeDtypeStruct((B, N_HEADS, HEAD_DIM, HEAD_DIM), F32),
        jax.ShapeDtypeStruct((B, N_HEADS, 1, HEAD_DIM), F32),
        jax.ShapeDtypeStruct((B, N_HEADS, 1, LANE), F32),
    ]
    return in_specs, out_specs, out_shape


def _hgrn_level_operand(lv, bc, g, kk, q_all, L):
    width = bc.shape[-1]
    rowi = lax.broadcasted_iota(jnp.int32, (L, width), 0)
    first_half = (rowi // lv) % 2 == 0
    if lv == 1:
        e = jnp.where(first_half, 0.0, g)
    elif 2 * lv >= SUBLANES:
        b3 = bc.reshape(L // (2 * lv), 2 * lv, width)
        e = (b3 - b3[:, lv - 1:lv, :]).reshape(L, width)
    else:
        b3 = bc.reshape(L // SUBLANES, SUBLANES, width)
        sub = lax.broadcasted_iota(jnp.int32, b3.shape, 1)
        rho = b3[:, lv - 1:lv, :]
        for start in range(2 * lv, SUBLANES, 2 * lv):
            rho = jnp.where(sub >= start, b3[:, start + lv - 1:start + lv, :], rho)
        e = (b3 - rho).reshape(L, width)
    return (jnp.where(first_half, kk, q_all) * jnp.exp2(-jnp.abs(e))).astype(BF16)


def _hgrn_kernel(q_ref, f_ref, i_ref, la_ref, l1_ref, oml_ref, lvl_ref, s0_ref, o_ref, st_ref, *, L):
    c = pl.program_id(1)

    @pl.when(c == 0)
    def _():
        st_ref[...] = s0_ref[...]

    f = f_ref[...]
    la = la_ref[...]
    lb_term = l1_ref[...] + _log_sigmoid(f)
    g = jnp.maximum(la, lb_term) + _softplus_neg_abs(la - lb_term)
    kk = oml_ref[...] / (1.0 + jnp.exp(f))
    q_all = q_ref[...]

    row = lax.broadcasted_iota(jnp.int32, (L, L), 0)
    col = lax.broadcasted_iota(jnp.int32, (L, L), 1)
    ltri = jnp.where(col <= row, 1.0, 0.0).astype(BF16)
    g = g * LOG2_E
    bc = sum(_dot(ltri, piece.astype(BF16)) for piece in _split3(g))
    bc_last = bc[L - 1:L, :]
    qe = (q_all * jnp.exp2(bc)).astype(BF16)
    kdec = (kk * jnp.exp2(bc_last - bc)).astype(BF16)
    dec_state = jnp.exp2(bc_last)
    n_levels = L.bit_length() - 1
    operands = [_hgrn_level_operand(1 << p, bc, g, kk, q_all, L) for p in range(n_levels)]
    lvl = lvl_ref[...]

    for h in range(N_HEADS):
        sl = slice(h * HEAD_DIM, (h + 1) * HEAD_DIM)
        ih = i_ref[:, sl]
        ib = ih.astype(BF16)
        st = st_ref[h]
        a = jnp.zeros((L, L), F32)
        for p, y in enumerate(operands):
            a = jnp.where(lvl == p, _dot(y[:, sl], y[:, sl], NT_DIMS), a)
        diag = jnp.sum(q_all[:, sl] * kk[:, sl], axis=-1, keepdims=True)
        o_ref[:, sl] = _dot(a.astype(BF16), ib) + _dot(qe[:, sl], st.astype(BF16), NT_DIMS) + diag * ih
        st_ref[h] = st * dec_state[:, sl] + _dot(ib, kdec[:, sl], TN_DIMS)


def _hgrn_level_index(L):
    t = np.arange(L)[:, None]
    s = np.arange(L)[None, :]
    x = np.maximum(t ^ s, 1)
    return np.where(s < t, np.floor(np.log2(x)), -1).astype(np.int32)


def _hgrn_specs(B, T, L):
    assert L & (L - 1) == 0 and L >= SUBLANES
    wide = lambda blk: pl.BlockSpec((None, L, BRANCH_WIDTH), lambda b, c: (b, c, blk))
    rowspec = pl.BlockSpec((1, BRANCH_WIDTH), lambda b, c: (0, 0))
    state = pl.BlockSpec((None, N_HEADS, HEAD_DIM, HEAD_DIM), lambda b, c: (b, 0, 0, 0))
    in_specs = [wide(C_HGQ // 4), wide(C_HGF // 4), wide(C_HGI // 4), rowspec, rowspec, rowspec,
                pl.BlockSpec((L, L), lambda b, c: (0, 0)), state]
    out_specs = [pl.BlockSpec((None, L, BRANCH_WIDTH), lambda b, c: (b, c, 0)), state]
    out_shape = [
        jax.ShapeDtypeStruct((B, T, BRANCH_WIDTH), F32),
        jax.ShapeDtypeStruct((B, N_HEADS, HEAD_DIM, HEAD_DIM), F32),
    ]
    return in_specs, out_specs, out_shape


def _recurrent_kernel(*refs, L, n_in, n_out):
    m_in, h_in = refs[:n_in[0]], refs[n_in[0]:sum(n_in)]
    outs = refs[sum(n_in):sum(n_in) + sum(n_out)]
    m_out, h_out = outs[:n_out[0]], outs[n_out[0]:]
    xs_ref = refs[-1]
    _mlstm_kernel(*m_in, *m_out, xs_ref, L=L)
    _hgrn_kernel(*h_in, *h_out, L=L)


def _recurrent(proj, gates, conv0_pad, w_conv, bias_row, c0, n0, m0, la, l1, oml, st0):
    B, T, _ = proj.shape
    L = min(RECURRENT_CHUNK, T)
    m_specs, h_specs = _mlstm_specs(B, T, L), _hgrn_specs(B, T, L)
    n_in, n_out = (len(m_specs[0]), len(h_specs[0])), (len(m_specs[1]), len(h_specs[1]))
    return pl.pallas_call(
        functools.partial(_recurrent_kernel, L=L, n_in=n_in, n_out=n_out),
        grid=(B, T // L),
        in_specs=m_specs[0] + h_specs[0],
        out_specs=m_specs[1] + h_specs[1],
        out_shape=m_specs[2] + h_specs[2],
        scratch_shapes=[pltpu.VMEM((L + HALO, D_MODEL), F32)],
        compiler_params=pltpu.CompilerParams(
            dimension_semantics=("parallel", "arbitrary"), vmem_limit_bytes=VMEM_LIMIT),
        name="recurrent",
    )(proj, proj, proj, gates, conv0_pad, w_conv, bias_row, c0, n0, m0,
      proj, proj, proj, la, l1, oml, jnp.asarray(_hgrn_level_index(L)), st0)


SB_KEY_BLOCK = 256
SB_QUERY_BLOCK = 512
SB_ZERO_WEIGHT_LOG = -104.0


def _strict_upper(n):
    r = lax.broadcasted_iota(jnp.int32, (n, n), 0)
    c = lax.broadcasted_iota(jnp.int32, (n, n), 1)
    return jnp.where(r > c, 1.0, 0.0).astype(BF16)


def _sb_block(qb, kb, vb, carry, acc, upper, key_offset=None, valid=None):
    z = _dot(qb, kb, NT_DIMS)
    sp = jnp.maximum(z, 0.0) + _softplus_neg_abs(z)
    if valid is not None:
        mask = valid
        lp = jnp.where(mask, -sp, 0.0)
    elif key_offset is None:
        mask = None
        lp = -sp
    else:
        r = lax.broadcasted_iota(jnp.int32, z.shape, 0)
        c = lax.broadcasted_iota(jnp.int32, z.shape, 1)
        mask = c + key_offset < r
        lp = jnp.where(mask, -sp, 0.0)
    lp_hi = lp.astype(BF16)
    lp_lo = (lp - lp_hi.astype(F32)).astype(BF16)
    suffix = _dot(lp_hi, upper) + _dot(lp_lo, upper)
    a = jnp.exp((z - sp) + suffix + carry)
    if mask is not None:
        a = jnp.where(mask, a, 0.0)
    acc = acc + _dot(a.astype(BF16), vb)
    carry = carry + jnp.sum(lp, axis=-1, keepdims=True)
    return carry, acc


def _sb_walk(qb, carry, acc, n_blocks, load_block, tk):
    upper = _strict_upper(tk)

    def cond(state):
        t, carry, _ = state
        return jnp.logical_and(t < n_blocks, jnp.max(carry) > SB_ZERO_WEIGHT_LOG)

    def body(state):
        t, carry, acc = state
        kb, vb = load_block(n_blocks - 1 - t)
        carry, acc = _sb_block(qb, kb, vb, carry, acc, upper)
        return t + 1, carry, acc

    _, carry, acc = lax.while_loop(cond, body, (jnp.int32(0), carry, acc))
    return acc


def _sb_prompt_kernel(q_ref, k_ref, v_ref, o_ref, kb_ref, vb_ref, *, tq, tk):
    i = pl.program_id(2)
    T = k_ref.shape[0]
    cast_rows = min(T, 1024)

    @pl.when(i == 0)
    def _():
        def cast(t, _):
            rows = pl.ds(pl.multiple_of(t * cast_rows, cast_rows), cast_rows)
            kb_ref[rows, :] = k_ref[rows, :].astype(BF16)
            vb_ref[rows, :] = v_ref[rows, :].astype(BF16)
            return 0
        lax.fori_loop(0, T // cast_rows, cast, 0)

    def load_block(j):
        rows = pl.ds(pl.multiple_of(j * tk, tk), tk)
        return kb_ref[rows, :], vb_ref[rows, :]

    upper = _strict_upper(tk)
    groups = []
    for s in range(tq // tk):
        blk = i * (tq // tk) + s
        qb = (q_ref[s * tk:(s + 1) * tk, :] * (HEAD_DIM ** -0.5)).astype(BF16)
        carry, acc = _sb_block(qb, *load_block(blk), jnp.zeros((tk, 1), F32), jnp.zeros((tk, HEAD_DIM), F32), upper,
                               key_offset=0)
        carry, acc = _sb_block(qb, *load_block(jnp.maximum(blk - 1, 0)), carry, acc, upper, valid=blk >= 1)
        groups.append((qb, carry, acc, jnp.maximum(blk - 1, 0)))
    for s, (qb, carry, acc, n_before) in enumerate(groups):
        o_ref[s * tk:(s + 1) * tk, :] = _sb_walk(qb, carry, acc, n_before, load_block, tk)


def _sb_sample_kernel(q_ref, kc_ref, vc_ref, kp_ref, vp_ref, o_ref, *, tk):
    tq = q_ref.shape[0]
    qb = (q_ref[...] * (HEAD_DIM ** -0.5)).astype(BF16)
    carry, acc = _sb_block(qb, kc_ref[...].astype(BF16), vc_ref[...].astype(BF16), jnp.zeros((tq, 1), F32),
                           jnp.zeros((tq, HEAD_DIM), F32), _strict_upper(tq), key_offset=0)

    def load_block(j):
        rows = pl.ds(pl.multiple_of(j * tk, tk), tk)
        return kp_ref[rows, :].astype(BF16), vp_ref[rows, :].astype(BF16)

    o_ref[...] = _sb_walk(qb, carry, acc, kp_ref.shape[0] // tk, load_block, tk)


def _sb_attention(l, proj, k_rows, v_rows, k_past, v_past):
    B, T, _ = proj.shape
    per_head = lambda rows: pl.BlockSpec((None, None, None, rows, HEAD_DIM), lambda b, h, i: (l, b, h, 0, 0))
    if k_past is None:
        tq = min(T, SB_QUERY_BLOCK)
        kernel_fn = functools.partial(_sb_prompt_kernel, tq=tq, tk=min(tq, SB_KEY_BLOCK))
        past_specs, past_args = [], []
        scratch = [pltpu.VMEM((T, HEAD_DIM), BF16), pltpu.VMEM((T, HEAD_DIM), BF16)]
    else:
        tq = T
        P = k_past.shape[3]
        kernel_fn = functools.partial(_sb_sample_kernel, tk=min(P, SB_KEY_BLOCK))
        past_specs, past_args = [per_head(P), per_head(P)], [k_past, v_past]
        scratch = []
    return pl.pallas_call(
        kernel_fn,
        grid=(B, N_HEADS, T // tq),
        in_specs=[pl.BlockSpec((None, tq, HEAD_DIM), lambda b, h, i: (b, i, C_SBQ + h)), per_head(T), per_head(T)]
        + past_specs,
        out_specs=pl.BlockSpec((None, tq, HEAD_DIM), lambda b, h, i: (b, i, h)),
        out_shape=jax.ShapeDtypeStruct((B, T, BRANCH_WIDTH), F32),
        scratch_shapes=scratch,
        compiler_params=pltpu.CompilerParams(
            dimension_semantics=("parallel", "parallel", "arbitrary"), vmem_limit_bytes=VMEM_LIMIT),
        name="stickbreak",
    )(proj, k_rows, v_rows, *past_args)


def _merge_kernel(hm_ref, os_ref, oh_ref, x_ref, p_ref, gpre_ref, wg0_ref, wg1_ref, wg2_ref, gmlh_ref, ghgh_ref,
                  wb_ref, wo_ref, gpost_ref, wpg_ref, wpp_ref, r_ref):
    wg_refs = (wg0_ref, wg1_ref, wg2_ref)

    def head_norm(a):
        parts = []
        for h in range(N_HEADS):
            seg = a[:, h * HEAD_DIM:(h + 1) * HEAD_DIM]
            parts.append(seg * lax.rsqrt(jnp.mean(seg * seg, axis=-1, keepdims=True) + RMS_EPS))
        return jnp.concatenate(parts, axis=-1)

    x = x_ref[...]
    u = _rms_norm_bf16(x, gpre_ref[...])
    def proj(where, width, skip=0):
        run, start = where
        return _dot(u, wg_refs[run][:, start + skip:start + skip + width])

    silu = lambda a: a * _sigmoid(a)
    h_ml = head_norm(hm_ref[...]) * gmlh_ref[...] * _sigmoid(proj(G_MLO, W)) * silu(proj(G_MLZ, W))
    h_sb = os_ref[...] * silu(proj(G_SBZ, W))
    h_hg = head_norm(oh_ref[...]) * ghgh_ref[...] * silu(proj(G_HGZ, W))
    merged = (_sigmoid(proj(G_GATES, D_MODEL)) * _dot(h_ml.astype(BF16), wb_ref[0])
              + _sigmoid(proj(G_GATES, D_MODEL, D_MODEL)) * _dot(h_sb.astype(BF16), wb_ref[1])
              + _sigmoid(proj(G_GATES, D_MODEL, 2 * D_MODEL)) * _dot(h_hg.astype(BF16), wb_ref[2]))
    y = _dot(merged.astype(BF16), wo_ref[...])
    r = x + y * lax.rsqrt(jnp.mean(y * y, axis=-1, keepdims=True) + RMS_EPS) * gpost_ref[...]
    gate = _sigmoid(_dot(r.astype(BF16), wpg_ref[...]))
    r_ref[...] = r + gate * _dot(p_ref[...].astype(BF16), wpp_ref[...])


def _merge(l, hm, o_s, o_h, x2d, p_all, lw):
    n = x2d.shape[0]
    tm = min(n, 256)
    rows = lambda width: pl.BlockSpec((tm, width), lambda i: (i, 0))
    full = lambda a: pl.BlockSpec(a.shape, lambda i: (0,) * a.ndim)
    params = [lw["g_pre"], *lw["w_gate"]] + [lw[k] for k in ("g_mlh", "g_hgh", "w_branch", "w_out", "g_post", "w_pg",
                                                              "w_pp")]
    return pl.pallas_call(
        _merge_kernel,
        grid=(n // tm,),
        in_specs=[rows(W), rows(W), rows(W), rows(D_MODEL), pl.BlockSpec((None, tm, PLE_DIM), lambda i: (l, i, 0))]
        + [full(a) for a in params],
        out_specs=rows(D_MODEL),
        out_shape=jax.ShapeDtypeStruct((n, D_MODEL), F32),
        compiler_params=pltpu.CompilerParams(
            dimension_semantics=("parallel",), vmem_limit_bytes=VMEM_LIMIT),
        name="merge",
    )(hm, o_s, o_h, x2d, p_all, *params)


def _layer(l, depth, x, p_all, c0, n0, m0, conv0, s0, k_past, v_past, kv_prev, lw):
    B, T, _ = x.shape
    proj2d, gates2d, k_rows, v_rows = _inproj(l, depth, x, lw["g_pre"], lw["w_in"], lw["w_if"], kv_prev)
    proj = proj2d.reshape(B, T, N_PROJ)

    conv0_pad = jnp.pad(conv0, ((0, 0), (HALO - (CONV_WIDTH - 1), 0), (0, 0)))
    hm, c_new, n_new, m_new, o_h, st_new = _recurrent(
        proj, gates2d.reshape(B, T, 2 * LANE), conv0_pad, lw["w_conv"], lw["bias_row"], c0,
        n0.reshape(B, N_HEADS, 1, HEAD_DIM),
        jnp.broadcast_to(m0[:, :, None, None], (B, N_HEADS, 1, LANE)),
        lw["log_lb"], lw["log1m_lb"], lw["one_m_lb"], jnp.swapaxes(s0, -1, -2))
    o_s = _sb_attention(l, proj, k_rows, v_rows, k_past, v_past)

    flat = lambda a: a.reshape(B * T, a.shape[-1])
    r = _merge(l, flat(hm), flat(o_s), flat(o_h), flat(x), p_all.reshape(depth, B * T, PLE_DIM), lw)

    conv_new = proj[:, T - (CONV_WIDTH - 1):, C_MLQ * LANE:C_MLQ * LANE + 2 * BRANCH_WIDTH]
    states = (c_new, n_new[:, :, 0, :], m_new[:, :, 0, 0], conv_new, jnp.swapaxes(st_new, -1, -2))
    return r.reshape(B, T, D_MODEL), states, (k_rows, v_rows)


def _layer_weights(l, lower_bounds, g_pre, w_in, b_i, b_f, w_conv, g_mlh, g_hgh, w_branch, w_out, g_post, w_pg, w_pp):
    cols = lambda span: w_in[l, :, span[0]:span[1]].astype(BF16)
    lane_pad = lambda a: jnp.pad(a, ((0, 0), (0, LANE - a.shape[1])))
    w_if = jnp.concatenate([lane_pad(cols(SRC_ML_I)), lane_pad(cols(SRC_ML_F))], axis=1)
    w_re = jnp.stack([cols(span) for span in INPROJ_SOURCES])
    w_gate = [cols(span) for span in MERGE_SOURCES]
    bias_row = jnp.concatenate([jnp.pad(b_i[l], (0, LANE - N_HEADS)), jnp.pad(b_f[l], (0, LANE - N_HEADS))])
    bias_row = bias_row.reshape(1, 2 * LANE)
    lb = lower_bounds[l].reshape(1, BRANCH_WIDTH)
    return dict(
        g_pre=g_pre[l].reshape(1, D_MODEL), w_in=w_re, w_if=w_if, w_gate=w_gate, bias_row=bias_row,
        w_conv=w_conv[l],
        g_mlh=g_mlh[l].reshape(1, BRANCH_WIDTH), g_hgh=g_hgh[l].reshape(1, BRANCH_WIDTH),
        log_lb=jnp.log(lb), log1m_lb=jnp.log1p(-lb), one_m_lb=1.0 - lb,
        w_branch=w_branch[l].astype(BF16), w_out=w_out[l].astype(BF16), g_post=g_post[l].reshape(1, D_MODEL),
        w_pg=w_pg[l].astype(BF16), w_pp=w_pp[l].astype(BF16))


def kernel(x_prompt, x_sample, p_prompt, p_sample, state_mlstm_C, state_mlstm_n, state_mlstm_m, state_mlstm_conv,
           cache_sb_k, cache_sb_v, state_hgrn_S, g_pre, w_in, b_mlstm_i, b_mlstm_f, w_mlstm_conv, g_mlstm_head,
           hgrn_lb_logits, g_hgrn_head, w_branch, w_out, g_post, w_ple_gate, w_ple_proj):
    depth = w_in.shape[0]
    lb_cum = jnp.cumsum(jax.nn.softmax(hgrn_lb_logits.astype(F32), axis=0), axis=0)
    lower_bounds = lb_cum - lb_cum[:1]
    B = x_prompt.shape[0]
    yp, ys = x_prompt, x_sample
    new_p, new_s = [], []
    kv_p = kv_s = None
    for l in range(depth):
        lw = _layer_weights(l, lower_bounds, g_pre, w_in, b_mlstm_i, b_mlstm_f, w_mlstm_conv, g_mlstm_head,
                            g_hgrn_head, w_branch, w_out, g_post, w_ple_gate, w_ple_proj)
        zeros = lambda *shape: jnp.zeros(shape, F32)
        yp, st, kv_p = _layer(l, depth, yp, p_prompt, zeros(B, N_HEADS, HEAD_DIM, HEAD_DIM),
                              zeros(B, N_HEADS, HEAD_DIM), zeros(B, N_HEADS),
                              zeros(B, CONV_WIDTH - 1, 2 * BRANCH_WIDTH), zeros(B, N_HEADS, HEAD_DIM, HEAD_DIM),
                              None, None, kv_p, lw)
        new_p.append(st)
        ys, st, kv_s = _layer(l, depth, ys, p_sample, state_mlstm_C[l], state_mlstm_n[l], state_mlstm_m[l],
                              state_mlstm_conv[l], state_hgrn_S[l], cache_sb_k, cache_sb_v, kv_s, lw)
        new_s.append(st)
    stack = lambda sts, i: jnp.stack([s[i] for s in sts])
    outs = lambda sts, kv: (stack(sts, 0), stack(sts, 1), stack(sts, 2), stack(sts, 3), kv[0], kv[1], stack(sts, 4))
    return (yp, ys, *outs(new_p, kv_p), *outs(new_s, kv_s))
```

```python
import functools

import jax
import jax.numpy as jnp
import numpy as np
from jax import lax
from jax.experimental import pallas as pl
from jax.experimental.pallas import tpu as pltpu

F32 = jnp.float32
BF16 = jnp.bfloat16

D_MODEL = 1024
PLE_DIM = 256
N_HEADS = 4
BRANCH_WIDTH = D_MODEL // 2
HEAD_DIM = BRANCH_WIDTH // N_HEADS
CONV_WIDTH = 4
RMS_EPS = 1e-6
LANE = 128
SUBLANES = 8
HALO = SUBLANES
NEG_BIG = -1e30
LOG2_E = 1.4426950408889634
RECURRENT_CHUNK = 128
RECURRENT_BLOCK = 256
MERGE_ROWS = 512

W = BRANCH_WIDTH
SRC_ML_QKV = (0, 3 * W)
SRC_ML_OZ = (3 * W, 5 * W)
SRC_ML_I = (5 * W, 5 * W + N_HEADS)
SRC_ML_F = (5 * W + N_HEADS, 5 * W + 2 * N_HEADS)
_S = 5 * W + 2 * N_HEADS
SRC_SB_QKV = (_S, _S + 3 * W)
SRC_SB_Z = (_S + 3 * W, _S + 4 * W)
SRC_HG_QFI = (_S + 4 * W, _S + 7 * W)
SRC_HG_Z_GATES = (_S + 7 * W, _S + 8 * W + 3 * D_MODEL)

INPROJ_TILE = 3 * BRANCH_WIDTH
INPROJ_SOURCES = (SRC_ML_QKV, SRC_HG_QFI, SRC_SB_QKV)
SB_TILE = 2
N_PROJ = SB_TILE * INPROJ_TILE
C_MLQ, C_MLK, C_MLV, C_HGQ, C_HGF, C_HGI = 0, 4, 8, 12, 16, 20
MERGE_SOURCES = (SRC_ML_OZ, SRC_SB_Z, SRC_HG_Z_GATES)
G_MLO, G_MLZ, G_SBZ, G_HGZ, G_GATES = (0, 0), (0, W), (1, 0), (2, 0), (2, W)

VMEM_LIMIT = 56 * 1024 * 1024

NT_DIMS = (((1,), (1,)), ((), ()))
TN_DIMS = (((0,), (0,)), ((), ()))


def _sigmoid(x):
    return 1.0 / (1.0 + jnp.exp(-x))


def _softplus_neg_abs(x):
    return jnp.log(1.0 + jnp.exp(-jnp.abs(x)))


def _log_sigmoid(x):
    return jnp.minimum(x, 0.0) - _softplus_neg_abs(x)


def _dot(a, b, dims=None):
    if dims is None:
        dims = (((a.ndim - 1,), (0,)), ((), ()))
    return lax.dot_general(a, b, dims, preferred_element_type=F32)


def _split3(x):
    hi = x.astype(BF16).astype(F32)
    r = x - hi
    mid = r.astype(BF16).astype(F32)
    return hi, mid, r - mid


def _rms_norm_bf16(x, g):
    ms = jnp.mean(x * x, axis=-1, keepdims=True)
    return (x * lax.rsqrt(ms + RMS_EPS) * g).astype(BF16)


def _inproj_kernel(*refs, aliased):
    x_ref, g_ref, w_ref, wif_ref = refs[:4]
    o_ref, gi_ref, sbq_ref, k_ref, v_ref, u_ref = refs[-6:]
    j = pl.program_id(1)

    @pl.when(j == 0)
    def _():
        u_ref[...] = _rms_norm_bf16(x_ref[...], g_ref[...])
        gi_ref[...] = jnp.dot(u_ref[...], wif_ref[...], preferred_element_type=F32)

    res = jnp.dot(u_ref[...], w_ref[...], preferred_element_type=F32)

    @pl.when(j < SB_TILE)
    def _():
        o_ref[...] = res

    @pl.when(j == SB_TILE)
    def _():
        sbq_ref[...] = (res[:, :W] * (HEAD_DIM ** -0.5)).astype(BF16)
        if aliased:
            k_dst, v_dst = k_ref, v_ref
        else:
            k_dst, v_dst = k_ref.at[0], v_ref.at[0]
            if k_ref.shape[0] > 1:
                k_ref[1:] = jnp.zeros((k_ref.shape[0] - 1,) + k_ref.shape[1:], F32)
                v_ref[1:] = jnp.zeros((v_ref.shape[0] - 1,) + v_ref.shape[1:], F32)
        bt, _, tt, _ = k_dst.shape
        for h in range(N_HEADS):
            k_dst[:, h] = res[:, W + h * HEAD_DIM:W + (h + 1) * HEAD_DIM].reshape(bt, tt, HEAD_DIM)
            v_dst[:, h] = res[:, 2 * W + h * HEAD_DIM:2 * W + (h + 1) * HEAD_DIM].reshape(bt, tt, HEAD_DIM)


def _inproj(l, depth, x, g_pre, w_tiles, w_if, kv_prev):
    B, T, _ = x.shape
    n = B * T
    tm = min(n, 1024)
    bt, tt = max(1, tm // T), min(tm, T)
    aliased = kv_prev is not None
    assert aliased == (l > 0)
    kv_index = lambda i, j: ((i * tm) // T // bt, 0, ((i * tm) % T) // tt, 0)
    if aliased:
        kv_block = pl.BlockSpec((None, bt, N_HEADS, tt, HEAD_DIM), lambda i, j: (l,) + kv_index(i, j))
    else:
        kv_block = pl.BlockSpec((depth, bt, N_HEADS, tt, HEAD_DIM), lambda i, j: (0,) + kv_index(i, j))
    kv_shape = jax.ShapeDtypeStruct((depth, B, N_HEADS, T, HEAD_DIM), F32)
    prev_specs = [pl.BlockSpec(memory_space=pl.ANY)] * 2 if aliased else []
    return pl.pallas_call(
        functools.partial(_inproj_kernel, aliased=aliased),
        grid=(n // tm, len(INPROJ_SOURCES)),
        in_specs=[
            pl.BlockSpec((tm, D_MODEL), lambda i, j: (i, 0)),
            pl.BlockSpec((1, D_MODEL), lambda i, j: (0, 0)),
            pl.BlockSpec((None, D_MODEL, INPROJ_TILE), lambda i, j: (j, 0, 0)),
            pl.BlockSpec((D_MODEL, 2 * LANE), lambda i, j: (0, 0)),
        ] + prev_specs,
        out_specs=[
            pl.BlockSpec((tm, INPROJ_TILE), lambda i, j: (i, jnp.minimum(j, SB_TILE - 1))),
            pl.BlockSpec((tm, 2 * LANE), lambda i, j: (i, 0)),
            pl.BlockSpec((tm, W), lambda i, j: (i, 0)),
            kv_block, kv_block,
        ],
        out_shape=[jax.ShapeDtypeStruct((n, N_PROJ), F32), jax.ShapeDtypeStruct((n, 2 * LANE), F32),
                   jax.ShapeDtypeStruct((n, W), BF16), kv_shape, kv_shape],
        input_output_aliases={4: 3, 5: 4} if aliased else {},
        scratch_shapes=[pltpu.VMEM((tm, D_MODEL), BF16)],
        compiler_params=pltpu.CompilerParams(
            dimension_semantics=("parallel", "arbitrary"), vmem_limit_bytes=VMEM_LIMIT),
        name="inproj",
    )(x.reshape(n, D_MODEL), g_pre.reshape(1, D_MODEL), w_tiles, w_if, *(kv_prev if aliased else ()))


def _mlstm_chunk(q_ref, k_ref, v_ref, gi_ref, wc_ref, bias_ref, hm_ref, c_ref, n_ref, m_ref, xs_ref, *, L, rows):
    xs_ref[HALO:HALO + L, 0:BRANCH_WIDTH] = q_ref[rows, :]
    xs_ref[HALO:HALO + L, BRANCH_WIDTH:] = k_ref[rows, :]
    base = HALO - (CONV_WIDTH - 1)
    qk = xs_ref[base:base + L, :] * wc_ref[0:1, :]
    for j in range(1, CONV_WIDTH):
        qk = qk + xs_ref[base + j:base + j + L, :] * wc_ref[j:j + 1, :]
    qk = qk * _sigmoid(qk)
    xs_ref[0:HALO, :] = xs_ref[L:L + HALO, :]

    gates = gi_ref[rows, :] + bias_ref[...]
    ig_all = gates[:, :LANE]
    row = lax.broadcasted_iota(jnp.int32, (L, L), 0)
    col = lax.broadcasted_iota(jnp.int32, (L, L), 1)
    causal = col <= row
    ltri = jnp.where(causal, 1.0, 0.0).astype(BF16)
    bcum = sum(_dot(ltri, piece.astype(BF16)) for piece in _split3(_log_sigmoid(gates[:, LANE:])))
    key_term = jnp.transpose(ig_all - bcum)
    scale = HEAD_DIM ** -0.5

    for h in range(N_HEADS):
        ig = ig_all[:, h:h + 1]
        b = bcum[:, h:h + 1]
        dmat = jnp.where(causal, b + key_term[h:h + 1, :], NEG_BIG)
        m_prev = m_ref[h][:, 0:1]
        inter = b + m_prev
        m_t = jnp.maximum(jnp.max(dmat, axis=-1, keepdims=True), inter)
        qh = qk[:, h * HEAD_DIM:(h + 1) * HEAD_DIM]
        kh = qk[:, BRANCH_WIDTH + h * HEAD_DIM:BRANCH_WIDTH + (h + 1) * HEAD_DIM] * scale
        vh = v_ref[rows, h * HEAD_DIM:(h + 1) * HEAD_DIM]
        qb, kb, vb = qh.astype(BF16), kh.astype(BF16), vh.astype(BF16)
        s = _dot(qb, kb, NT_DIMS) * jnp.exp(dmat - m_t)
        w_inter = jnp.exp(inter - m_t)
        c_prev = c_ref[h]
        n_prev = n_ref[h]
        num = _dot(s.astype(BF16), vb) + w_inter * _dot(qb, c_prev.astype(BF16))
        den = jnp.sum(s, axis=-1, keepdims=True) + w_inter * jnp.sum(qh * n_prev, axis=-1, keepdims=True)
        hm_ref[rows, h * HEAD_DIM:(h + 1) * HEAD_DIM] = num /jnp.maximum(jnp.abs(den), jnp.exp(-m_t))
        m_new = m_t[L - 1:L, :]
        b_last = b[L - 1:L, :]
        dec_s = jnp.exp(b_last - b + ig - m_new)
        dec_c = jnp.exp(b_last + m_prev - m_new)
        kd = kh * dec_s
        c_ref[h] = dec_c * c_prev + _dot(kd.astype(BF16), vb, TN_DIMS)
        n_ref[h] = dec_c * n_prev + jnp.sum(kd, axis=0, keepdims=True)
        m_ref[h] = jnp.broadcast_to(m_new, (1, LANE))


def _hgrn_level_operand(lv, bc, g, kk, q_all, L):
    width = bc.shape[-1]
    rowi = lax.broadcasted_iota(jnp.int32, (L, width), 0)
    first_half = (rowi // lv) % 2 == 0
    if lv == 1:
        e = jnp.where(first_half, 0.0, g)
    elif 2 * lv >= SUBLANES:
        b3 = bc.reshape(L // (2 * lv), 2 * lv, width)
        e = (b3 - b3[:, lv - 1:lv, :]).reshape(L, width)
    else:
        b3 = bc.reshape(L // SUBLANES, SUBLANES, width)
        sub = lax.broadcasted_iota(jnp.int32, b3.shape, 1)
        rho = b3[:, lv - 1:lv, :]
        for start in range(2 * lv, SUBLANES, 2 * lv):
            rho = jnp.where(sub >= start, b3[:, start + lv - 1:start + lv, :], rho)
        e = (b3 - rho).reshape(L, width)
    return (jnp.where(first_half, kk, q_all) * jnp.exp2(-jnp.abs(e))).astype(BF16)


def _hgrn_chunk(q_ref, f_ref, i_ref, la_ref, l1_ref, oml_ref, lvl_ref, o_ref, st_ref, *, L, rows):
    f = f_ref[rows, :]
    la = la_ref[...]
    lb_term = l1_ref[...] + _log_sigmoid(f)
    g = jnp.maximum(la, lb_term) + _softplus_neg_abs(la - lb_term)
    kk = oml_ref[...] / (1.0 + jnp.exp(f))
    q_all = q_ref[rows, :]

    row = lax.broadcasted_iota(jnp.int32, (L, L), 0)
    col = lax.broadcasted_iota(jnp.int32, (L, L), 1)
    ltri = jnp.where(col <= row, 1.0, 0.0).astype(BF16)
    g = g * LOG2_E
    bc = sum(_dot(ltri, piece.astype(BF16)) for piece in _split3(g))
    bc_last = bc[L - 1:L, :]
    qe = (q_all * jnp.exp2(bc)).astype(BF16)
    kdec = (kk * jnp.exp2(bc_last - bc)).astype(BF16)
    dec_state = jnp.exp2(bc_last)
    n_levels = L.bit_length() - 1
    operands = [_hgrn_level_operand(1 << p, bc, g, kk, q_all, L) for p in range(n_levels)]
    lvl = lvl_ref[...]

    for h in range(N_HEADS):
        sl = slice(h * HEAD_DIM, (h + 1) * HEAD_DIM)
        ih = i_ref[rows, sl]
        ib = ih.astype(BF16)
        st = st_ref[h]
        a = jnp.zeros((L, L), F32)
        for p, y in enumerate(operands):
            a = jnp.where(lvl == p, _dot(y[:, sl], y[:, sl], NT_DIMS), a)
        diag = jnp.sum(q_all[:, sl] * kk[:, sl], axis=-1, keepdims=True)
        o_ref[rows, sl] = _dot(a.astype(BF16), ib) + _dot(qe[:, sl], st.astype(BF16), NT_DIMS) + diag * ih
        st_ref[h] = st * dec_state[:, sl] + _dot(ib, kdec[:, sl], TN_DIMS)


def _hgrn_level_index(L):
    t = np.arange(L)[:, None]
    s = np.arange(L)[None, :]
    x = np.maximum(t ^ s, 1)
    return np.where(s < t, np.floor(np.log2(x)), -1).astype(np.int32)


def _recurrent_kernel(mq_ref, mk_ref, mv_ref, gi_ref, hq_ref, hf_ref, hi_ref, conv0_ref, wc_ref, bias_ref,
                      c0_ref, n0_ref, m0_ref, la_ref, l1_ref, oml_ref, lvl_ref, s0_ref,
                      hm_ref, c_ref, n_ref, m_ref, oh_ref, st_ref, xs_ref, *, L):
    @pl.when(pl.program_id(1) == 0)
    def _():
        c_ref[...] = c0_ref[...]
        n_ref[...] = n0_ref[...]
        m_ref[...] = m0_ref[...]
        st_ref[...] = s0_ref[...]
        xs_ref[0:HALO, :] = conv0_ref[...]

    for s in range(mq_ref.shape[0] // L):
        rows = slice(s * L, (s + 1) * L)
        _mlstm_chunk(mq_ref, mk_ref, mv_ref, gi_ref, wc_ref, bias_ref, hm_ref, c_ref, n_ref, m_ref, xs_ref,
                     L=L, rows=rows)
        _hgrn_chunk(hq_ref, hf_ref, hi_ref, la_ref, l1_ref, oml_ref, lvl_ref, oh_ref, st_ref, L=L, rows=rows)


def _recurrent(proj, gates, conv0_pad, w_conv, bias_row, c0, n0, m0, la, l1, oml, st0):
    B, T, _ = proj.shape
    L = min(RECURRENT_CHUNK, T)
    rows = min(RECURRENT_BLOCK, T)
    assert L & (L - 1) == 0 and L >= SUBLANES
    wide = lambda blk: pl.BlockSpec((None, rows, BRANCH_WIDTH), lambda b, c: (b, c, blk))
    const = lambda a: pl.BlockSpec(a.shape, lambda b, c: (0,) * a.ndim)
    state = lambda *shp: pl.BlockSpec((None,) + shp, lambda b, c: (b,) + (0,) * len(shp))
    states = [state(N_HEADS, HEAD_DIM, HEAD_DIM), state(N_HEADS, 1, HEAD_DIM), state(N_HEADS, 1, LANE)]
    hgrn_state = state(N_HEADS, HEAD_DIM, HEAD_DIM)
    lvl = jnp.asarray(_hgrn_level_index(L))
    f32 = lambda *shp: jax.ShapeDtypeStruct(shp, F32)
    return pl.pallas_call(
        functools.partial(_recurrent_kernel, L=L),
        grid=(B, T // rows),
        in_specs=[wide(C_MLQ // 4), wide(C_MLK // 4), wide(C_MLV // 4),
                  pl.BlockSpec((None, rows, 2 * LANE), lambda b, c: (b, c, 0)),
                  wide(C_HGQ // 4), wide(C_HGF // 4), wide(C_HGI // 4),
                  state(HALO, D_MODEL), const(w_conv), const(bias_row), *states,
                  const(la), const(l1), const(oml), const(lvl), hgrn_state],
        out_specs=[wide(0), *states, wide(0), hgrn_state],
        out_shape=[f32(B, T, BRANCH_WIDTH), f32(B, N_HEADS, HEAD_DIM, HEAD_DIM), f32(B, N_HEADS, 1, HEAD_DIM),
                   f32(B, N_HEADS, 1, LANE), f32(B, T, BRANCH_WIDTH), f32(B, N_HEADS, HEAD_DIM, HEAD_DIM)],
        scratch_shapes=[pltpu.VMEM((L + HALO, D_MODEL), F32)],
        compiler_params=pltpu.CompilerParams(
            dimension_semantics=("parallel", "arbitrary"), vmem_limit_bytes=VMEM_LIMIT),
        name="recurrent",
    )(proj, proj, proj, gates, proj, proj, proj, conv0_pad, w_conv, bias_row, c0, n0, m0, la, l1, oml, lvl, st0)


SB_KEY_BLOCK = 256
SB_QUERY_BLOCK = 1024
SB_ZERO_WEIGHT_LOG = -104.0


def _strict_upper(n):
    r = lax.broadcasted_iota(jnp.int32, (n, n), 0)
    c = lax.broadcasted_iota(jnp.int32, (n, n), 1)
    return jnp.where(r > c, 1.0, 0.0).astype(BF16)


def _sb_block(qb, kb, vb, carry, acc, upper, key_offset=None, valid=None):
    z = _dot(qb, kb, NT_DIMS)
    sp = jnp.maximum(z, 0.0) + _softplus_neg_abs(z)
    if valid is not None:
        mask = valid
        lp = jnp.where(mask, -sp, 0.0)
    elif key_offset is None:
        mask = None
        lp = -sp
    else:
        r = lax.broadcasted_iota(jnp.int32, z.shape, 0)
        c = lax.broadcasted_iota(jnp.int32, z.shape, 1)
        mask = c + key_offset < r
        lp = jnp.where(mask, -sp, 0.0)
    lp_hi = lp.astype(BF16)
    lp_lo = (lp - lp_hi.astype(F32)).astype(BF16)
    suffix = _dot(lp_hi, upper) + _dot(lp_lo, upper)
    a = jnp.exp((z - sp) + suffix + carry)
    if mask is not None:
        a = jnp.where(mask, a, 0.0)
    acc = acc + _dot(a.astype(BF16), vb)
    carry = carry + jnp.sum(lp, axis=-1, keepdims=True)
    return carry, acc


def _sb_walk(qb, carry, acc, n_blocks, load_block, tk):
    upper = _strict_upper(tk)

    def cond(state):
        t, carry, _ = state
        return jnp.logical_and(t < n_blocks, jnp.max(carry) > SB_ZERO_WEIGHT_LOG)

    def body(state):
        t, carry, acc = state
        kb, vb = load_block(n_blocks - 1 - t)
        carry, acc = _sb_block(qb, kb, vb, carry, acc, upper)
        return t + 1, carry, acc

    _, carry, acc = lax.while_loop(cond, body, (jnp.int32(0), carry, acc))
    return acc


def _sb_prompt_kernel(q_ref, k_ref, v_ref, o_ref, kb_ref, vb_ref, *, tq, tk):
    i = pl.program_id(2)
    T = k_ref.shape[0]
    cast_rows = min(T, 1024)

    @pl.when(i == 0)
    def _():
        def cast(t, _):
            rows = pl.ds(pl.multiple_of(t * cast_rows, cast_rows), cast_rows)
            kb_ref[rows, :] = k_ref[rows, :].astype(BF16)
            vb_ref[rows, :] = v_ref[rows, :].astype(BF16)
            return 0
        lax.fori_loop(0, T // cast_rows, cast, 0)

    def load_block(j):
        rows = pl.ds(pl.multiple_of(j * tk, tk), tk)
        return kb_ref[rows, :], vb_ref[rows, :]

    upper = _strict_upper(tk)
    groups = []
    for s in range(tq // tk):
        blk = i * (tq // tk) + s
        qb = q_ref[s * tk:(s + 1) * tk, :]
        carry, acc = _sb_block(qb, *load_block(blk), jnp.zeros((tk, 1), F32), jnp.zeros((tk, HEAD_DIM), F32), upper,
                               key_offset=0)
        carry, acc = _sb_block(qb, *load_block(jnp.maximum(blk - 1, 0)), carry, acc, upper, valid=blk >= 1)
        groups.append((qb, carry, acc, jnp.maximum(blk - 1, 0)))
    for s, (qb, carry, acc, n_before) in enumerate(groups):
        o_ref[s * tk:(s + 1) * tk, :] = _sb_walk(qb, carry, acc, n_before, load_block, tk)


def _sb_sample_kernel(q_ref, kc_ref, vc_ref, kp_ref, vp_ref, o_ref, *, tk):
    tq = q_ref.shape[0]
    qb = q_ref[...]
    carry, acc = _sb_block(qb, kc_ref[...].astype(BF16), vc_ref[...].astype(BF16), jnp.zeros((tq, 1), F32),
                           jnp.zeros((tq, HEAD_DIM), F32), _strict_upper(tq), key_offset=0)

    def load_block(j):
        rows = pl.ds(pl.multiple_of(j * tk, tk), tk)
        return kp_ref[rows, :].astype(BF16), vp_ref[rows, :].astype(BF16)

    o_ref[...] = _sb_walk(qb, carry, acc, kp_ref.shape[0] // tk, load_block, tk)


def _sb_attention(l, q_scaled, k_rows, v_rows, k_past, v_past):
    B, T, _ = q_scaled.shape
    per_head = lambda rows: pl.BlockSpec((None, None, None, rows, HEAD_DIM), lambda b, h, i: (l, b, h, 0, 0))
    if k_past is None:
        tq = min(T, SB_QUERY_BLOCK)
        kernel_fn = functools.partial(_sb_prompt_kernel, tq=tq, tk=min(tq, SB_KEY_BLOCK))
        past_specs, past_args = [], []
        scratch = [pltpu.VMEM((T, HEAD_DIM), BF16), pltpu.VMEM((T, HEAD_DIM), BF16)]
    else:
        tq = T
        P = k_past.shape[3]
        kernel_fn = functools.partial(_sb_sample_kernel, tk=min(P, SB_KEY_BLOCK))
        past_specs, past_args = [per_head(P), per_head(P)], [k_past, v_past]
        scratch = []
    return pl.pallas_call(
        kernel_fn,
        grid=(B, N_HEADS, T // tq),
        in_specs=[pl.BlockSpec((None, tq, HEAD_DIM), lambda b, h, i: (b, i, h)), per_head(T), per_head(T)]
        + past_specs,
        out_specs=pl.BlockSpec((None, tq, HEAD_DIM), lambda b, h, i: (b, i, h)),
        out_shape=jax.ShapeDtypeStruct((B, T, BRANCH_WIDTH), F32),
        scratch_shapes=scratch,
        compiler_params=pltpu.CompilerParams(
            dimension_semantics=("parallel", "parallel", "arbitrary"), vmem_limit_bytes=VMEM_LIMIT),
        name="stickbreak",
    )(q_scaled, k_rows, v_rows, *past_args)


def _merge_kernel(hm_ref, os_ref, oh_ref, x_ref, p_ref, gpre_ref, wg0_ref, wg1_ref, wg2_ref, gmlh_ref, ghgh_ref,
                  wb_ref, wo_ref, gpost_ref, wpg_ref, wpp_ref, r_ref):
    wg_refs = (wg0_ref, wg1_ref, wg2_ref)

    def head_norm(a):
        parts = []
        for h in range(N_HEADS):
            seg = a[:, h * HEAD_DIM:(h + 1) * HEAD_DIM]
            parts.append(seg * lax.rsqrt(jnp.mean(seg * seg, axis=-1, keepdims=True) + RMS_EPS))
        return jnp.concatenate(parts, axis=-1)

    x = x_ref[...]
    u = _rms_norm_bf16(x, gpre_ref[...])
    def proj(where, width, skip=0):
        run, start = where
        return _dot(u, wg_refs[run][:, start + skip:start + skip + width])

    silu = lambda a: a * _sigmoid(a)
    h_ml = head_norm(hm_ref[...]) * gmlh_ref[...] * _sigmoid(proj(G_MLO, W)) * silu(proj(G_MLZ, W))
    h_sb = os_ref[...] * silu(proj(G_SBZ, W))
    h_hg = head_norm(oh_ref[...]) * ghgh_ref[...] * silu(proj(G_HGZ, W))
    merged = (_sigmoid(proj(G_GATES, D_MODEL)) * _dot(h_ml.astype(BF16), wb_ref[0])
              + _sigmoid(proj(G_GATES, D_MODEL, D_MODEL)) * _dot(h_sb.astype(BF16), wb_ref[1])
              + _sigmoid(proj(G_GATES, D_MODEL, 2 * D_MODEL)) * _dot(h_hg.astype(BF16), wb_ref[2]))
    y = _dot(merged.astype(BF16), wo_ref[...])
    r = x + y * lax.rsqrt(jnp.mean(y * y, axis=-1, keepdims=True) + RMS_EPS) * gpost_ref[...]
    gate = _sigmoid(_dot(r.astype(BF16), wpg_ref[...]))
    r_ref[...] = r + gate * _dot(p_ref[...].astype(BF16), wpp_ref[...])


def _merge(l, hm, o_s, o_h, x2d, p_all, lw):
    n = x2d.shape[0]
    tm = min(n, MERGE_ROWS)
    rows = lambda width: pl.BlockSpec((tm, width), lambda i: (i, 0))
    full = lambda a: pl.BlockSpec(a.shape, lambda i: (0,) * a.ndim, pipeline_mode=pl.Buffered(1))
    params = [lw["g_pre"], *lw["w_gate"]] + [lw[k] for k in ("g_mlh", "g_hgh", "w_branch", "w_out", "g_post", "w_pg",
                                                              "w_pp")]
    return pl.pallas_call(
        _merge_kernel,
        grid=(n // tm,),
        in_specs=[rows(W), rows(W), rows(W), rows(D_MODEL), pl.BlockSpec((None, tm, PLE_DIM), lambda i: (l, i, 0))]
        + [full(a) for a in params],
        out_specs=rows(D_MODEL),
        out_shape=jax.ShapeDtypeStruct((n, D_MODEL), F32),
        compiler_params=pltpu.CompilerParams(
            dimension_semantics=("parallel",), vmem_limit_bytes=VMEM_LIMIT),
        name="merge",
    )(hm, o_s, o_h, x2d, p_all, *params)


def _layer(l, depth, x, p_all, c0, n0, m0, conv0, s0, k_past, v_past, kv_prev, lw):
    B, T, _ = x.shape
    proj2d, gates2d, sbq2d, k_rows, v_rows = _inproj(l, depth, x, lw["g_pre"], lw["w_in"], lw["w_if"], kv_prev)
    proj = proj2d.reshape(B, T, N_PROJ)

    conv0_pad = jnp.pad(conv0, ((0, 0), (HALO - (CONV_WIDTH - 1), 0), (0, 0)))
    hm, c_new, n_new, m_new, o_h, st_new = _recurrent(
        proj, gates2d.reshape(B, T, 2 * LANE), conv0_pad, lw["w_conv"], lw["bias_row"], c0,
        n0.reshape(B, N_HEADS, 1, HEAD_DIM),
        jnp.broadcast_to(m0[:, :, None, None], (B, N_HEADS, 1, LANE)),
        lw["log_lb"], lw["log1m_lb"], lw["one_m_lb"], jnp.swapaxes(s0, -1, -2))
    o_s = _sb_attention(l, sbq2d.reshape(B, T, BRANCH_WIDTH), k_rows, v_rows, k_past, v_past)

    flat = lambda a: a.reshape(B * T, a.shape[-1])
    r = _merge(l, flat(hm), flat(o_s), flat(o_h), flat(x), p_all.reshape(depth, B * T, PLE_DIM), lw)

    conv_new = proj[:, T - (CONV_WIDTH - 1):, C_MLQ * LANE:C_MLQ * LANE + 2 * BRANCH_WIDTH]
    states = (c_new, n_new[:, :, 0, :], m_new[:, :, 0, 0], conv_new, jnp.swapaxes(st_new, -1, -2))
    return r.reshape(B, T, D_MODEL), states, (k_rows, v_rows)


def _layer_weights(l, lower_bounds, g_pre, w_in, b_i, b_f, w_conv, g_mlh, g_hgh, w_branch, w_out, g_post, w_pg, w_pp):
    cols = lambda span: w_in[l, :, span[0]:span[1]].astype(BF16)
    lane_pad = lambda a: jnp.pad(a, ((0, 0), (0, LANE - a.shape[1])))
    w_if = jnp.concatenate([lane_pad(cols(SRC_ML_I)), lane_pad(cols(SRC_ML_F))], axis=1)
    w_re = jnp.stack([cols(span) for span in INPROJ_SOURCES])
    w_gate = [cols(span) for span in MERGE_SOURCES]
    bias_row = jnp.concatenate([jnp.pad(b_i[l], (0, LANE - N_HEADS)), jnp.pad(b_f[l], (0, LANE - N_HEADS))])
    bias_row = bias_row.reshape(1, 2 * LANE)
    lb = lower_bounds[l].reshape(1, BRANCH_WIDTH)
    return dict(
        g_pre=g_pre[l].reshape(1, D_MODEL), w_in=w_re, w_if=w_if, w_gate=w_gate, bias_row=bias_row,
        w_conv=w_conv[l],
        g_mlh=g_mlh[l].reshape(1, BRANCH_WIDTH), g_hgh=g_hgh[l].reshape(1, BRANCH_WIDTH),
        log_lb=jnp.log(lb), log1m_lb=jnp.log1p(-lb), one_m_lb=1.0 - lb,
        w_branch=w_branch[l].astype(BF16), w_out=w_out[l].astype(BF16), g_post=g_post[l].reshape(1, D_MODEL),
        w_pg=w_pg[l].astype(BF16), w_pp=w_pp[l].astype(BF16))


def kernel(x_prompt, x_sample, p_prompt, p_sample, state_mlstm_C, state_mlstm_n, state_mlstm_m, state_mlstm_conv,
           cache_sb_k, cache_sb_v, state_hgrn_S, g_pre, w_in, b_mlstm_i, b_mlstm_f, w_mlstm_conv, g_mlstm_head,
           hgrn_lb_logits, g_hgrn_head, w_branch, w_out, g_post, w_ple_gate, w_ple_proj):
    depth = w_in.shape[0]
    lb_cum = jnp.cumsum(jax.nn.softmax(hgrn_lb_logits.astype(F32), axis=0), axis=0)
    lower_bounds = lb_cum - lb_cum[:1]
    B = x_prompt.shape[0]
    yp, ys = x_prompt, x_sample
    new_p, new_s = [], []
    kv_p = kv_s = None
    for l in range(depth):
        lw = _layer_weights(l, lower_bounds, g_pre, w_in, b_mlstm_i, b_mlstm_f, w_mlstm_conv, g_mlstm_head,
                            g_hgrn_head, w_branch, w_out, g_post, w_ple_gate, w_ple_proj)
        zeros = lambda *shape: jnp.zeros(shape, F32)
        yp, st, kv_p = _layer(l, depth, yp, p_prompt, zeros(B, N_HEADS, HEAD_DIM, HEAD_DIM),
                              zeros(B, N_HEADS, HEAD_DIM), zeros(B, N_HEADS),
                              zeros(B, CONV_WIDTH - 1, 2 * BRANCH_WIDTH), zeros(B, N_HEADS, HEAD_DIM, HEAD_DIM),
                              None, None, kv_p, lw)
        new_p.append(st)
        ys, st, kv_s = _layer(l, depth, ys, p_sample, state_mlstm_C[l], state_mlstm_n[l], state_mlstm_m[l],
                              state_mlstm_conv[l], state_hgrn_S[l], cache_sb_k, cache_sb_v, kv_s, lw)
        new_s.append(st)
    stack = lambda sts, i: jnp.stack([s[i] for s in sts])
    outs = lambda sts, kv: (stack(sts, 0), stack(sts, 1), stack(sts, 2), stack(sts, 3), kv[0], kv[1], stack(sts, 4))
    return (yp, ys, *outs(new_p, kv_p), *outs(new_s, kv_s))
```

```python
import functools

import jax
import jax.numpy as jnp
import numpy as np
from jax import lax
from jax.experimental import pallas as pl
from jax.experimental.pallas import tpu as pltpu

F32 = jnp.float32
BF16 = jnp.bfloat16

D_MODEL = 1024
PLE_DIM = 256
N_HEADS = 4
BRANCH_WIDTH = D_MODEL // 2
HEAD_DIM = BRANCH_WIDTH // N_HEADS
CONV_WIDTH = 4
RMS_EPS = 1e-6
LANE = 128
SUBLANES = 8
HALO = SUBLANES
NEG_BIG = -1e30
LOG2_E = 1.4426950408889634
RECURRENT_CHUNK = 128
RECURRENT_BLOCK = 512
MERGE_ROWS = 512

W = BRANCH_WIDTH
SRC_ML_QKV = (0, 3 * W)
SRC_ML_OZ = (3 * W, 5 * W)
SRC_ML_I = (5 * W, 5 * W + N_HEADS)
SRC_ML_F = (5 * W + N_HEADS, 5 * W + 2 * N_HEADS)
_S = 5 * W + 2 * N_HEADS
SRC_SB_QKV = (_S, _S + 3 * W)
SRC_SB_Z = (_S + 3 * W, _S + 4 * W)
SRC_HG_QFI = (_S + 4 * W, _S + 7 * W)
SRC_HG_Z_GATES = (_S + 7 * W, _S + 8 * W + 3 * D_MODEL)

INPROJ_TILE = 3 * BRANCH_WIDTH
INPROJ_SOURCES = (SRC_ML_QKV, SRC_HG_QFI, SRC_SB_QKV)
SB_TILE = 2
N_PROJ = SB_TILE * INPROJ_TILE
C_MLQ, C_MLK, C_MLV, C_HGQ, C_HGF, C_HGI = 0, 4, 8, 12, 16, 20
MERGE_SOURCES = (SRC_ML_OZ, SRC_SB_Z, SRC_HG_Z_GATES)
G_MLO, G_MLZ, G_SBZ, G_HGZ, G_GATES = (0, 0), (0, W), (1, 0), (2, 0), (2, W)

VMEM_LIMIT = 56 * 1024 * 1024

NT_DIMS = (((1,), (1,)), ((), ()))
TN_DIMS = (((0,), (0,)), ((), ()))


def _sigmoid(x):
    return 1.0 / (1.0 + jnp.exp(-x))


def _softplus_neg_abs(x):
    return jnp.log(1.0 + jnp.exp(-jnp.abs(x)))


def _log2_1p_exp2_neg_abs(x):
    return jnp.log2(1.0 + jnp.exp2(-jnp.abs(x)))


def _log_sigmoid(x):
    return jnp.minimum(x, 0.0) - _softplus_neg_abs(x)


def _dot(a, b, dims=None):
    if dims is None:
        dims = (((a.ndim - 1,), (0,)), ((), ()))
    return lax.dot_general(a, b, dims, preferred_element_type=F32)


def _split3(x):
    hi = x.astype(BF16).astype(F32)
    r = x - hi
    mid = r.astype(BF16).astype(F32)
    return hi, mid, r - mid


def _rms_norm_bf16(x, g):
    ms = jnp.mean(x * x, axis=-1, keepdims=True)
    return (x * lax.rsqrt(ms + RMS_EPS) * g).astype(BF16)


def _inproj_kernel(*refs, aliased):
    x_ref, g_ref, w_ref, wif_ref = refs[:4]
    o_ref, gi_ref, sbq_ref, k_ref, v_ref, u_ref = refs[-6:]
    j = pl.program_id(1)

    @pl.when(j == 0)
    def _():
        u_ref[...] = _rms_norm_bf16(x_ref[...], g_ref[...])
        gi_ref[...] = jnp.dot(u_ref[...], wif_ref[...], preferred_element_type=F32)

    res = jnp.dot(u_ref[...], w_ref[...], preferred_element_type=F32)

    @pl.when(j < SB_TILE)
    def _():
        o_ref[...] = res

    @pl.when(j == SB_TILE)
    def _():
        sbq_ref[...] = (res[:, :W] * (HEAD_DIM ** -0.5 * LOG2_E)).astype(BF16)
        if aliased:
            k_dst, v_dst = k_ref, v_ref
        else:
            k_dst, v_dst = k_ref.at[0], v_ref.at[0]
            if k_ref.shape[0] > 1:
                k_ref[1:] = jnp.zeros((k_ref.shape[0] - 1,) + k_ref.shape[1:], F32)
                v_ref[1:] = jnp.zeros((v_ref.shape[0] - 1,) + v_ref.shape[1:], F32)
        bt, _, tt, _ = k_dst.shape
        for h in range(N_HEADS):
            k_dst[:, h] = res[:, W + h * HEAD_DIM:W + (h + 1) * HEAD_DIM].reshape(bt, tt, HEAD_DIM)
            v_dst[:, h] = res[:, 2 * W + h * HEAD_DIM:2 * W + (h + 1) * HEAD_DIM].reshape(bt, tt, HEAD_DIM)


def _inproj(l, depth, x, g_pre, w_tiles, w_if, kv_prev):
    B, T, _ = x.shape
    n = B * T
    tm = min(n, 1024)
    bt, tt = max(1, tm // T), min(tm, T)
    aliased = kv_prev is not None
    assert aliased == (l > 0)
    kv_index = lambda i, j: ((i * tm) // T // bt, 0, ((i * tm) % T) // tt, 0)
    if aliased:
        kv_block = pl.BlockSpec((None, bt, N_HEADS, tt, HEAD_DIM), lambda i, j: (l,) + kv_index(i, j))
    else:
        kv_block = pl.BlockSpec((depth, bt, N_HEADS, tt, HEAD_DIM), lambda i, j: (0,) + kv_index(i, j))
    kv_shape = jax.ShapeDtypeStruct((depth, B, N_HEADS, T, HEAD_DIM), F32)
    prev_specs = [pl.BlockSpec(memory_space=pl.ANY)] * 2 if aliased else []
    return pl.pallas_call(
        functools.partial(_inproj_kernel, aliased=aliased),
        grid=(n // tm, len(INPROJ_SOURCES)),
        in_specs=[
            pl.BlockSpec((tm, D_MODEL), lambda i, j: (i, 0)),
            pl.BlockSpec((1, D_MODEL), lambda i, j: (0, 0)),
            pl.BlockSpec((None, D_MODEL, INPROJ_TILE), lambda i, j: (j, 0, 0)),
            pl.BlockSpec((D_MODEL, 2 * LANE), lambda i, j: (0, 0)),
        ] + prev_specs,
        out_specs=[
            pl.BlockSpec((tm, INPROJ_TILE), lambda i, j: (i, jnp.minimum(j, SB_TILE - 1))),
            pl.BlockSpec((tm, 2 * LANE), lambda i, j: (i, 0)),
            pl.BlockSpec((tm, W), lambda i, j: (i, 0)),
            kv_block, kv_block,
        ],
        out_shape=[jax.ShapeDtypeStruct((n, N_PROJ), F32), jax.ShapeDtypeStruct((n, 2 * LANE), F32),
                   jax.ShapeDtypeStruct((n, W), BF16), kv_shape, kv_shape],
        input_output_aliases={4: 3, 5: 4} if aliased else {},
        scratch_shapes=[pltpu.VMEM((tm, D_MODEL), BF16)],
        compiler_params=pltpu.CompilerParams(
            dimension_semantics=("parallel", "arbitrary"), vmem_limit_bytes=VMEM_LIMIT),
        name="inproj",
    )(x.reshape(n, D_MODEL), g_pre.reshape(1, D_MODEL), w_tiles, w_if, *(kv_prev if aliased else ()))


def _mlstm_chunk(q_ref, k_ref, v_ref, gi_ref, wc_ref, bias_ref, hm_ref, c_ref, n_ref, m_ref, xs_ref, *, L, rows):
    xs_ref[HALO:HALO + L, 0:BRANCH_WIDTH] = q_ref[rows, :]
    xs_ref[HALO:HALO + L, BRANCH_WIDTH:] = k_ref[rows, :]
    base = HALO - (CONV_WIDTH - 1)
    qk = xs_ref[base:base + L, :] * wc_ref[0:1, :]
    for j in range(1, CONV_WIDTH):
        qk = qk + xs_ref[base + j:base + j + L, :] * wc_ref[j:j + 1, :]
    qk = qk * _sigmoid(qk)
    xs_ref[0:HALO, :] = xs_ref[L:L + HALO, :]

    gates = gi_ref[rows, :] + bias_ref[...]
    ig_all = gates[:, :LANE]
    row = lax.broadcasted_iota(jnp.int32, (L, L), 0)
    col = lax.broadcasted_iota(jnp.int32, (L, L), 1)
    causal = col <= row
    ltri = jnp.where(causal, 1.0, 0.0).astype(BF16)
    bcum = sum(_dot(ltri, piece.astype(BF16)) for piece in _split3(_log_sigmoid(gates[:, LANE:])))
    key_term = jnp.transpose(ig_all - bcum)
    scale = HEAD_DIM ** -0.5

    for h in range(N_HEADS):
        ig = ig_all[:, h:h + 1]
        b = bcum[:, h:h + 1]
        dmat = jnp.where(causal, b + key_term[h:h + 1, :], NEG_BIG)
        m_prev = m_ref[h][:, 0:1]
        inter = b + m_prev
        m_t = jnp.maximum(jnp.max(dmat, axis=-1, keepdims=True), inter)
        qh = qk[:, h * HEAD_DIM:(h + 1) * HEAD_DIM]
        kh = qk[:, BRANCH_WIDTH + h * HEAD_DIM:BRANCH_WIDTH + (h + 1) * HEAD_DIM] * scale
        vh = v_ref[rows, h * HEAD_DIM:(h + 1) * HEAD_DIM]
        qb, kb, vb = qh.astype(BF16), kh.astype(BF16), vh.astype(BF16)
        s = _dot(qb, kb, NT_DIMS) * jnp.exp(dmat - m_t)
        w_inter = jnp.exp(inter - m_t)
        c_prev = c_ref[h]
        n_prev = n_ref[h]
        num = _dot(s.astype(BF16), vb) + w_inter * _dot(qb, c_prev.astype(BF16))
        den = jnp.sum(s, axis=-1, keepdims=True) + w_inter * jnp.sum(qh * n_prev, axis=-1, keepdims=True)
        hm_ref[rows, h * HEAD_DIM:(h + 1) * HEAD_DIM] = num /jnp.maximum(jnp.abs(den), jnp.exp(-m_t))
        m_new = m_t[L - 1:L, :]
        b_last = b[L - 1:L, :]
        dec_s = jnp.exp(b_last - b + ig - m_new)
        dec_c = jnp.exp(b_last + m_prev - m_new)
        kd = kh * dec_s
        c_ref[h] = dec_c * c_prev + _dot(kd.astype(BF16), vb, TN_DIMS)
        n_ref[h] = dec_c * n_prev + jnp.sum(kd, axis=0, keepdims=True)
        m_ref[h] = jnp.broadcast_to(m_new, (1, LANE))


def _hgrn_level_operand(lv, bc, g, kk, q_all, L):
    width = bc.shape[-1]
    rowi = lax.broadcasted_iota(jnp.int32, (L, width), 0)
    first_half = (rowi // lv) % 2 == 0
    if lv == 1:
        e = jnp.where(first_half, 0.0, g)
    elif 2 * lv >= SUBLANES:
        b3 = bc.reshape(L // (2 * lv), 2 * lv, width)
        e = (b3 - b3[:, lv - 1:lv, :]).reshape(L, width)
    else:
        b3 = bc.reshape(L // SUBLANES, SUBLANES, width)
        sub = lax.broadcasted_iota(jnp.int32, b3.shape, 1)
        rho = b3[:, lv - 1:lv, :]
        for start in range(2 * lv, SUBLANES, 2 * lv):
            rho = jnp.where(sub >= start, b3[:, start + lv - 1:start + lv, :], rho)
        e = (b3 - rho).reshape(L, width)
    return (jnp.where(first_half, kk, q_all) * jnp.exp2(-jnp.abs(e))).astype(BF16)


def _hgrn_chunk(q_ref, f_ref, i_ref, la_ref, l1_ref, oml_ref, lvl_ref, o_ref, st_ref, *, L, rows):
    f = f_ref[rows, :]
    la = la_ref[...]
    f = f * LOG2_E
    lb_term = l1_ref[...] + (jnp.minimum(f, 0.0) - _log2_1p_exp2_neg_abs(f))
    g = jnp.maximum(la, lb_term) + _log2_1p_exp2_neg_abs(la - lb_term)
    kk = oml_ref[...] / (1.0 + jnp.exp2(f))
    q_all = q_ref[rows, :]

    row = lax.broadcasted_iota(jnp.int32, (L, L), 0)
    col = lax.broadcasted_iota(jnp.int32, (L, L), 1)
    ltri = jnp.where(col <= row, 1.0, 0.0).astype(BF16)
    bc = sum(_dot(ltri, piece.astype(BF16)) for piece in _split3(g))
    bc_last = bc[L - 1:L, :]
    qe = (q_all * jnp.exp2(bc)).astype(BF16)
    kdec = (kk * jnp.exp2(bc_last - bc)).astype(BF16)
    dec_state = jnp.exp2(bc_last)
    n_levels = L.bit_length() - 1
    operands = [_hgrn_level_operand(1 << p, bc, g, kk, q_all, L) for p in range(n_levels)]
    lvl = lvl_ref[...]

    for h in range(N_HEADS):
        sl = slice(h * HEAD_DIM, (h + 1) * HEAD_DIM)
        ih = i_ref[rows, sl]
        ib = ih.astype(BF16)
        st = st_ref[h]
        a = jnp.zeros((L, L), F32)
        for p, y in enumerate(operands):
            a = jnp.where(lvl == p, _dot(y[:, sl], y[:, sl], NT_DIMS), a)
        diag = jnp.sum(q_all[:, sl] * kk[:, sl], axis=-1, keepdims=True)
        o_ref[rows, sl] = _dot(a.astype(BF16), ib) + _dot(qe[:, sl], st.astype(BF16), NT_DIMS) + diag * ih
        st_ref[h] = st * dec_state[:, sl] + _dot(ib, kdec[:, sl], TN_DIMS)


def _hgrn_level_index(L):
    t = np.arange(L)[:, None]
    s = np.arange(L)[None, :]
    x = np.maximum(t ^ s, 1)
    return np.where(s < t, np.floor(np.log2(x)), -1).astype(np.int32)


def _recurrent_kernel(mq_ref, mk_ref, mv_ref, gi_ref, hq_ref, hf_ref, hi_ref, conv0_ref, wc_ref, bias_ref,
                      c0_ref, n0_ref, m0_ref, la_ref, l1_ref, oml_ref, lvl_ref, s0_ref,
                      hm_ref, c_ref, n_ref, m_ref, oh_ref, st_ref, xs_ref, *, L):
    @pl.when(pl.program_id(1) == 0)
    def _():
        c_ref[...] = c0_ref[...]
        n_ref[...] = n0_ref[...]
        m_ref[...] = m0_ref[...]
        st_ref[...] = s0_ref[...]
        xs_ref[0:HALO, :] = conv0_ref[...]

    for s in range(mq_ref.shape[0] // L):
        rows = slice(s * L, (s + 1) * L)
        _mlstm_chunk(mq_ref, mk_ref, mv_ref, gi_ref, wc_ref, bias_ref, hm_ref, c_ref, n_ref, m_ref, xs_ref,
                     L=L, rows=rows)
        _hgrn_chunk(hq_ref, hf_ref, hi_ref, la_ref, l1_ref, oml_ref, lvl_ref, oh_ref, st_ref, L=L, rows=rows)


def _recurrent(proj, gates, conv0_pad, w_conv, bias_row, c0, n0, m0, la, l1, oml, st0):
    B, T, _ = proj.shape
    L = min(RECURRENT_CHUNK, T)
    rows = min(RECURRENT_BLOCK, T)
    assert L & (L - 1) == 0 and L >= SUBLANES
    wide = lambda blk: pl.BlockSpec((None, rows, BRANCH_WIDTH), lambda b, c: (b, c, blk))
    const = lambda a: pl.BlockSpec(a.shape, lambda b, c: (0,) * a.ndim)
    state = lambda *shp: pl.BlockSpec((None,) + shp, lambda b, c: (b,) + (0,) * len(shp))
    states = [state(N_HEADS, HEAD_DIM, HEAD_DIM), state(N_HEADS, 1, HEAD_DIM), state(N_HEADS, 1, LANE)]
    hgrn_state = state(N_HEADS, HEAD_DIM, HEAD_DIM)
    lvl = jnp.asarray(_hgrn_level_index(L))
    f32 = lambda *shp: jax.ShapeDtypeStruct(shp, F32)
    return pl.pallas_call(
        functools.partial(_recurrent_kernel, L=L),
        grid=(B, T // rows),
        in_specs=[wide(C_MLQ // 4), wide(C_MLK // 4), wide(C_MLV // 4),
                  pl.BlockSpec((None, rows, 2 * LANE), lambda b, c: (b, c, 0)),
                  wide(C_HGQ // 4), wide(C_HGF // 4), wide(C_HGI // 4),
                  state(HALO, D_MODEL), const(w_conv), const(bias_row), *states,
                  const(la), const(l1), const(oml), const(lvl), hgrn_state],
        out_specs=[wide(0), *states, wide(0), hgrn_state],
        out_shape=[f32(B, T, BRANCH_WIDTH), f32(B, N_HEADS, HEAD_DIM, HEAD_DIM), f32(B, N_HEADS, 1, HEAD_DIM),
                   f32(B, N_HEADS, 1, LANE), f32(B, T, BRANCH_WIDTH), f32(B, N_HEADS, HEAD_DIM, HEAD_DIM)],
        scratch_shapes=[pltpu.VMEM((L + HALO, D_MODEL), F32)],
        compiler_params=pltpu.CompilerParams(
            dimension_semantics=("parallel", "arbitrary"), vmem_limit_bytes=VMEM_LIMIT),
        name="recurrent",
    )(proj, proj, proj, gates, proj, proj, proj, conv0_pad, w_conv, bias_row, c0, n0, m0, la, l1, oml, lvl, st0)


SB_KEY_BLOCK = 256
SB_RECENT_KEYS = 512
SB_QUERY_BLOCK = 1024
SB_ZERO_WEIGHT_LOG2 = -150.0


def _strict_upper(n):
    r = lax.broadcasted_iota(jnp.int32, (n, n), 0)
    c = lax.broadcasted_iota(jnp.int32, (n, n), 1)
    u = jnp.where(r > c, 1.0, 0.0).astype(BF16)
    return jnp.concatenate([u, u], axis=0)


def _sb_block(qb, kb, vb, carry, acc, upper, key_offset=None, valid=None):
    z = _dot(qb, kb, NT_DIMS)
    sp = jnp.maximum(z, 0.0) + _log2_1p_exp2_neg_abs(z)
    if valid is not None:
        mask = valid
        lp = jnp.where(mask, -sp, 0.0)
    elif key_offset is None:
        mask = None
        lp = -sp
    else:
        r = lax.broadcasted_iota(jnp.int32, z.shape, 0)
        c = lax.broadcasted_iota(jnp.int32, z.shape, 1)
        mask = c + key_offset < r
        lp = jnp.where(mask, -sp, 0.0)
    lp_hi = lp.astype(BF16)
    lp_lo = (lp - lp_hi.astype(F32)).astype(BF16)
    suffix = _dot(jnp.concatenate([lp_hi, lp_lo], axis=1), upper)
    a = jnp.exp2((z - sp) + suffix + carry)
    if mask is not None:
        a = jnp.where(mask, a, 0.0)
    acc = acc + _dot(a.astype(BF16), vb)
    carry = carry + jnp.sum(lp, axis=-1, keepdims=True)
    return carry, acc


def _sb_walk(qb, carry, acc, n_blocks, load_block, tk):
    upper = _strict_upper(tk)

    def cond(state):
        t, carry, _ = state
        return jnp.logical_and(t < n_blocks, jnp.max(carry) > SB_ZERO_WEIGHT_LOG2)

    def body(state):
        t, carry, acc = state
        kb, vb = load_block(n_blocks - 1 - t)
        carry, acc = _sb_block(qb, kb, vb, carry, acc, upper)
        return t + 1, carry, acc

    _, carry, acc = lax.while_loop(cond, body, (jnp.int32(0), carry, acc))
    return carry, acc


def _sb_prompt_kernel(q_ref, k_ref, v_ref, o_ref, kb_ref, vb_ref, *, tq, tk):
    i = pl.program_id(2)
    T = k_ref.shape[0]
    cast_rows = min(T, 1024)

    @pl.when(i == 0)
    def _():
        def cast(t, _):
            rows = pl.ds(pl.multiple_of(t * cast_rows, cast_rows), cast_rows)
            kb_ref[rows, :] = k_ref[rows, :].astype(BF16)
            vb_ref[rows, :] = v_ref[rows, :].astype(BF16)
            return 0
        lax.fori_loop(0, T // cast_rows, cast, 0)

    def load_block(j):
        rows = pl.ds(pl.multiple_of(j * tk, tk), tk)
        return kb_ref[rows, :], vb_ref[rows, :]

    upper = _strict_upper(tk)
    groups = []
    for s in range(tq // tk):
        blk = i * (tq // tk) + s
        qb = q_ref[s * tk:(s + 1) * tk, :]
        carry, acc = _sb_block(qb, *load_block(blk), jnp.zeros((tk, 1), F32), jnp.zeros((tk, HEAD_DIM), F32), upper,
                               key_offset=0)
        carry, acc = _sb_block(qb, *load_block(jnp.maximum(blk - 1, 0)), carry, acc, upper, valid=blk >= 1)
        groups.append((qb, carry, acc, jnp.maximum(blk - 1, 0)))
    for s, (qb, carry, acc, n_before) in enumerate(groups):
        o_ref[s * tk:(s + 1) * tk, :] = _sb_walk(qb, carry, acc, n_before, load_block, tk)[1]


def _sb_sample_kernel(q_ref, kc_ref, vc_ref, kr_ref, vr_ref, kp_hbm, vp_hbm, o_ref, kbuf_ref, vbuf_ref, sem, *, l, tk):
    tq = q_ref.shape[0]
    b, h = pl.program_id(0), pl.program_id(1)
    qb = q_ref[...]
    carry, acc = _sb_block(qb, kc_ref[...].astype(BF16), vc_ref[...].astype(BF16), jnp.zeros((tq, 1), F32),
                           jnp.zeros((tq, HEAD_DIM), F32), _strict_upper(tq), key_offset=0)

    def load_recent(j):
        rows = pl.ds(pl.multiple_of(j * tk, tk), tk)
        return kr_ref[rows, :].astype(BF16), vr_ref[rows, :].astype(BF16)

    carry, acc = _sb_walk(qb, carry, acc, kr_ref.shape[0] // tk, load_recent, tk)

    def load_older(j):
        rows = pl.ds(pl.multiple_of(j * tk, tk), tk)
        copies = [pltpu.make_async_copy(src.at[l, b, h, rows, :], dst, sem.at[slot])
                  for slot, (src, dst) in enumerate(((kp_hbm, kbuf_ref), (vp_hbm, vbuf_ref)))]
        for cp in copies:
            cp.start()
        for cp in copies:
            cp.wait()
        return kbuf_ref[...].astype(BF16), vbuf_ref[...].astype(BF16)

    n_older = (kp_hbm.shape[3] - kr_ref.shape[0]) // tk
    o_ref[...] = _sb_walk(qb, carry, acc, n_older, load_older, tk)[1]


def _sb_attention(l, q_scaled, k_rows, v_rows, k_past, v_past):
    B, T, _ = q_scaled.shape
    per_head = lambda rows: pl.BlockSpec((None, None, None, rows, HEAD_DIM), lambda b, h, i: (l, b, h, 0, 0))
    if k_past is None:
        tq = min(T, SB_QUERY_BLOCK)
        kernel_fn = functools.partial(_sb_prompt_kernel, tq=tq, tk=min(tq, SB_KEY_BLOCK))
        past_specs, past_args = [], []
        scratch = [pltpu.VMEM((T, HEAD_DIM), BF16), pltpu.VMEM((T, HEAD_DIM), BF16)]
    else:
        tq = T
        P = k_past.shape[3]
        tk, recent = min(P, SB_KEY_BLOCK), min(P, SB_RECENT_KEYS)
        kernel_fn = functools.partial(_sb_sample_kernel, l=l, tk=tk)
        newest = pl.BlockSpec((None, None, None, recent, HEAD_DIM), lambda b, h, i: (l, b, h, P // recent - 1, 0))
        in_hbm = pl.BlockSpec(memory_space=pl.ANY)
        past_specs, past_args = [newest, newest, in_hbm, in_hbm], [k_past, v_past, k_past, v_past]
        scratch = [pltpu.VMEM((tk, HEAD_DIM), F32), pltpu.VMEM((tk, HEAD_DIM), F32), pltpu.SemaphoreType.DMA((2,))]
    return pl.pallas_call(
        kernel_fn,
        grid=(B, N_HEADS, T // tq),
        in_specs=[pl.BlockSpec((None, tq, HEAD_DIM), lambda b, h, i: (b, i, h)), per_head(T), per_head(T)]
        + past_specs,
        out_specs=pl.BlockSpec((None, tq, HEAD_DIM), lambda b, h, i: (b, i, h)),
        out_shape=jax.ShapeDtypeStruct((B, T, BRANCH_WIDTH), F32),
        scratch_shapes=scratch,
        compiler_params=pltpu.CompilerParams(
            dimension_semantics=("parallel", "parallel", "arbitrary"), vmem_limit_bytes=VMEM_LIMIT),
        name="stickbreak",
    )(q_scaled, k_rows, v_rows, *past_args)


def _merge_kernel(hm_ref, os_ref, oh_ref, x_ref, p_ref, gpre_ref, wg0_ref, wg1_ref, wg2_ref, gmlh_ref, ghgh_ref,
                  wb_ref, wo_ref, gpost_ref, wpg_ref, wpp_ref, r_ref):
    wg_refs = (wg0_ref, wg1_ref, wg2_ref)

    def head_norm(a):
        parts = []
        for h in range(N_HEADS):
            seg = a[:, h * HEAD_DIM:(h + 1) * HEAD_DIM]
            parts.append(seg * lax.rsqrt(jnp.mean(seg * seg, axis=-1, keepdims=True) + RMS_EPS))
        return jnp.concatenate(parts, axis=-1)

    x = x_ref[...]
    u = _rms_norm_bf16(x, gpre_ref[...])
    def proj(where, width, skip=0):
        run, start = where
        return _dot(u, wg_refs[run][:, start + skip:start + skip + width])

    silu = lambda a: a * _sigmoid(a)
    h_ml = head_norm(hm_ref[...]) * gmlh_ref[...] * _sigmoid(proj(G_MLO, W)) * silu(proj(G_MLZ, W))
    h_sb = os_ref[...] * silu(proj(G_SBZ, W))
    h_hg = head_norm(oh_ref[...]) * ghgh_ref[...] * silu(proj(G_HGZ, W))
    merged = (_sigmoid(proj(G_GATES, D_MODEL)) * _dot(h_ml.astype(BF16), wb_ref[0])
              + _sigmoid(proj(G_GATES, D_MODEL, D_MODEL)) * _dot(h_sb.astype(BF16), wb_ref[1])
              + _sigmoid(proj(G_GATES, D_MODEL, 2 * D_MODEL)) * _dot(h_hg.astype(BF16), wb_ref[2]))
    y = _dot(merged.astype(BF16), wo_ref[...])
    r = x + y * lax.rsqrt(jnp.mean(y * y, axis=-1, keepdims=True) + RMS_EPS) * gpost_ref[...]
    gate = _sigmoid(_dot(r.astype(BF16), wpg_ref[...]))
    r_ref[...] = r + gate * _dot(p_ref[...].astype(BF16), wpp_ref[...])


def _merge(l, hm, o_s, o_h, x2d, p_all, lw):
    n = x2d.shape[0]
    tm = min(n, MERGE_ROWS)
    rows = lambda width: pl.BlockSpec((tm, width), lambda i: (i, 0))
    full = lambda a: pl.BlockSpec(a.shape, lambda i: (0,) * a.ndim, pipeline_mode=pl.Buffered(1))
    params = [lw["g_pre"], *lw["w_gate"]] + [lw[k] for k in ("g_mlh", "g_hgh", "w_branch", "w_out", "g_post", "w_pg",
                                                              "w_pp")]
    return pl.pallas_call(
        _merge_kernel,
        grid=(n // tm,),
        in_specs=[rows(W), rows(W), rows(W), rows(D_MODEL), pl.BlockSpec((None, tm, PLE_DIM), lambda i: (l, i, 0))]
        + [full(a) for a in params],
        out_specs=rows(D_MODEL),
        out_shape=jax.ShapeDtypeStruct((n, D_MODEL), F32),
        compiler_params=pltpu.CompilerParams(
            dimension_semantics=("parallel",), vmem_limit_bytes=VMEM_LIMIT),
        name="merge",
    )(hm, o_s, o_h, x2d, p_all, *params)


def _layer(l, depth, x, p_all, c0, n0, m0, conv0, s0, k_past, v_past, kv_prev, lw):
    B, T, _ = x.shape
    proj2d, gates2d, sbq2d, k_rows, v_rows = _inproj(l, depth, x, lw["g_pre"], lw["w_in"], lw["w_if"], kv_prev)
    proj = proj2d.reshape(B, T, N_PROJ)

    conv0_pad = jnp.pad(conv0, ((0, 0), (HALO - (CONV_WIDTH - 1), 0), (0, 0)))
    hm, c_new, n_new, m_new, o_h, st_new = _recurrent(
        proj, gates2d.reshape(B, T, 2 * LANE), conv0_pad, lw["w_conv"], lw["bias_row"], c0,
        n0.reshape(B, N_HEADS, 1, HEAD_DIM),
        jnp.broadcast_to(m0[:, :, None, None], (B, N_HEADS, 1, LANE)),
        lw["log_lb"], lw["log1m_lb"], lw["one_m_lb"], jnp.swapaxes(s0, -1, -2))
    o_s = _sb_attention(l, sbq2d.reshape(B, T, BRANCH_WIDTH), k_rows, v_rows, k_past, v_past)

    flat = lambda a: a.reshape(B * T, a.shape[-1])
    r = _merge(l, flat(hm), flat(o_s), flat(o_h), flat(x), p_all.reshape(depth, B * T, PLE_DIM), lw)

    conv_new = proj[:, T - (CONV_WIDTH - 1):, C_MLQ * LANE:C_MLQ * LANE + 2 * BRANCH_WIDTH]
    states = (c_new, n_new[:, :, 0, :], m_new[:, :, 0, 0], conv_new, jnp.swapaxes(st_new, -1, -2))
    return r.reshape(B, T, D_MODEL), states, (k_rows, v_rows)


def _layer_weights(l, lower_bounds, g_pre, w_in, b_i, b_f, w_conv, g_mlh, g_hgh, w_branch, w_out, g_post, w_pg, w_pp):
    cols = lambda span: w_in[l, :, span[0]:span[1]].astype(BF16)
    lane_pad = lambda a: jnp.pad(a, ((0, 0), (0, LANE - a.shape[1])))
    w_if = jnp.concatenate([lane_pad(cols(SRC_ML_I)), lane_pad(cols(SRC_ML_F))], axis=1)
    w_re = jnp.stack([cols(span) for span in INPROJ_SOURCES])
    w_gate = [cols(span) for span in MERGE_SOURCES]
    bias_row = jnp.concatenate([jnp.pad(b_i[l], (0, LANE - N_HEADS)), jnp.pad(b_f[l], (0, LANE - N_HEADS))])
    bias_row = bias_row.reshape(1, 2 * LANE)
    lb = lower_bounds[l].reshape(1, BRANCH_WIDTH)
    return dict(
        g_pre=g_pre[l].reshape(1, D_MODEL), w_in=w_re, w_if=w_if, w_gate=w_gate, bias_row=bias_row,
        w_conv=w_conv[l],
        g_mlh=g_mlh[l].reshape(1, BRANCH_WIDTH), g_hgh=g_hgh[l].reshape(1, BRANCH_WIDTH),
        log_lb=jnp.log(lb) * LOG2_E, log1m_lb=jnp.log1p(-lb) * LOG2_E, one_m_lb=1.0 - lb,
        w_branch=w_branch[l].astype(BF16), w_out=w_out[l].astype(BF16), g_post=g_post[l].reshape(1, D_MODEL),
        w_pg=w_pg[l].astype(BF16), w_pp=w_pp[l].astype(BF16))


def kernel(x_prompt, x_sample, p_prompt, p_sample, state_mlstm_C, state_mlstm_n, state_mlstm_m, state_mlstm_conv,
           cache_sb_k, cache_sb_v, state_hgrn_S, g_pre, w_in, b_mlstm_i, b_mlstm_f, w_mlstm_conv, g_mlstm_head,
           hgrn_lb_logits, g_hgrn_head, w_branch, w_out, g_post, w_ple_gate, w_ple_proj):
    depth = w_in.shape[0]
    lb_cum = jnp.cumsum(jax.nn.softmax(hgrn_lb_logits.astype(F32), axis=0), axis=0)
    lower_bounds = lb_cum - lb_cum[:1]
    B = x_prompt.shape[0]
    yp, ys = x_prompt, x_sample
    new_p, new_s = [], []
    kv_p = kv_s = None
    for l in range(depth):
        lw = _layer_weights(l, lower_bounds, g_pre, w_in, b_mlstm_i, b_mlstm_f, w_mlstm_conv, g_mlstm_head,
                            g_hgrn_head, w_branch, w_out, g_post, w_ple_gate, w_ple_proj)
        zeros = lambda *shape: jnp.zeros(shape, F32)
        yp, st, kv_p = _layer(l, depth, yp, p_prompt, zeros(B, N_HEADS, HEAD_DIM, HEAD_DIM),
                              zeros(B, N_HEADS, HEAD_DIM), zeros(B, N_HEADS),
                              zeros(B, CONV_WIDTH - 1, 2 * BRANCH_WIDTH), zeros(B, N_HEADS, HEAD_DIM, HEAD_DIM),
                              None, None, kv_p, lw)
        new_p.append(st)
        ys, st, kv_s = _layer(l, depth, ys, p_sample, state_mlstm_C[l], state_mlstm_n[l], state_mlstm_m[l],
                              state_mlstm_conv[l], state_hgrn_S[l], cache_sb_k, cache_sb_v, kv_s, lw)
        new_s.append(st)
    stack = lambda sts, i: jnp.stack([s[i] for s in sts])
    outs = lambda sts, kv: (stack(sts, 0), stack(sts, 1), stack(sts, 2), stack(sts, 3), kv[0], kv[1], stack(sts, 4))
    return (yp, ys, *outs(new_p, kv_p), *outs(new_s, kv_s))
```

```python
import functools

import jax
import jax.numpy as jnp
import numpy as np
from jax import lax
from jax.experimental import pallas as pl
from jax.experimental.pallas import tpu as pltpu

F32 = jnp.float32
BF16 = jnp.bfloat16

D_MODEL = 1024
PLE_DIM = 256
N_HEADS = 4
BRANCH_WIDTH = D_MODEL // 2
HEAD_DIM = BRANCH_WIDTH // N_HEADS
CONV_WIDTH = 4
RMS_EPS = 1e-6
LANE = 128
SUBLANES = 8
HALO = SUBLANES
NEG_BIG = -1e30
LOG2_E = 1.4426950408889634
RECURRENT_CHUNK = 128
RECURRENT_BLOCK = 1024
MERGE_ROWS = 512

W = BRANCH_WIDTH
SRC_ML_QKV = (0, 3 * W)
SRC_ML_OZ = (3 * W, 5 * W)
SRC_ML_I = (5 * W, 5 * W + N_HEADS)
SRC_ML_F = (5 * W + N_HEADS, 5 * W + 2 * N_HEADS)
_S = 5 * W + 2 * N_HEADS
SRC_SB_QKV = (_S, _S + 3 * W)
SRC_SB_Z = (_S + 3 * W, _S + 4 * W)
SRC_HG_QFI = (_S + 4 * W, _S + 7 * W)
SRC_HG_Z_GATES = (_S + 7 * W, _S + 8 * W + 3 * D_MODEL)

INPROJ_TILE = 3 * BRANCH_WIDTH
INPROJ_SOURCES = (SRC_ML_QKV, SRC_HG_QFI, SRC_SB_QKV)
SB_TILE = 2
N_PROJ = SB_TILE * INPROJ_TILE
C_MLQ, C_MLK, C_MLV, C_HGQ, C_HGF, C_HGI = 0, 4, 8, 12, 16, 20
MERGE_SOURCES = (SRC_ML_OZ, SRC_SB_Z, SRC_HG_Z_GATES)
G_MLO, G_MLZ, G_SBZ, G_HGZ, G_GATES = (0, 0), (0, W), (1, 0), (2, 0), (2, W)

VMEM_LIMIT = 56 * 1024 * 1024

NT_DIMS = (((1,), (1,)), ((), ()))
TN_DIMS = (((0,), (0,)), ((), ()))


def _sigmoid(x):
    return 1.0 / (1.0 + jnp.exp(-x))


def _softplus_neg_abs(x):
    return jnp.log(1.0 + jnp.exp(-jnp.abs(x)))


def _log2_1p_exp2_neg_abs(x):
    return jnp.log2(1.0 + jnp.exp2(-jnp.abs(x)))


def _log_sigmoid(x):
    return jnp.minimum(x, 0.0) - _softplus_neg_abs(x)


def _dot(a, b, dims=None):
    if dims is None:
        dims = (((a.ndim - 1,), (0,)), ((), ()))
    return lax.dot_general(a, b, dims, preferred_element_type=F32)


def _split3(x):
    hi = x.astype(BF16).astype(F32)
    r = x - hi
    mid = r.astype(BF16).astype(F32)
    return hi, mid, r - mid


def _rms_norm_bf16(x, g):
    ms = jnp.mean(x * x, axis=-1, keepdims=True)
    return (x * lax.rsqrt(ms + RMS_EPS) * g).astype(BF16)


def _inproj_kernel(*refs, aliased):
    x_ref, g_ref, w_ref, wif_ref = refs[:4]
    o_ref, gi_ref, sbq_ref, k_ref, v_ref, u_ref = refs[-6:]
    j = pl.program_id(1)

    @pl.when(j == 0)
    def _():
        u_ref[...] = _rms_norm_bf16(x_ref[...], g_ref[...])
        gi_ref[...] = jnp.dot(u_ref[...], wif_ref[...], preferred_element_type=F32)

    res = jnp.dot(u_ref[...], w_ref[...], preferred_element_type=F32)

    @pl.when(j < SB_TILE)
    def _():
        o_ref[...] = res

    @pl.when(j == SB_TILE)
    def _():
        sbq_ref[...] = (res[:, :W] * (HEAD_DIM ** -0.5 * LOG2_E)).astype(BF16)
        if aliased:
            k_dst, v_dst = k_ref, v_ref
        else:
            k_dst, v_dst = k_ref.at[0], v_ref.at[0]
            if k_ref.shape[0] > 1:
                k_ref[1:] = jnp.zeros((k_ref.shape[0] - 1,) + k_ref.shape[1:], F32)
                v_ref[1:] = jnp.zeros((v_ref.shape[0] - 1,) + v_ref.shape[1:], F32)
        bt, _, tt, _ = k_dst.shape
        for h in range(N_HEADS):
            k_dst[:, h] = res[:, W + h * HEAD_DIM:W + (h + 1) * HEAD_DIM].reshape(bt, tt, HEAD_DIM)
            v_dst[:, h] = res[:, 2 * W + h * HEAD_DIM:2 * W + (h + 1) * HEAD_DIM].reshape(bt, tt, HEAD_DIM)


def _inproj(l, depth, x, g_pre, w_tiles, w_if, kv_prev):
    B, T, _ = x.shape
    n = B * T
    tm = min(n, 1024)
    bt, tt = max(1, tm // T), min(tm, T)
    aliased = kv_prev is not None
    assert aliased == (l > 0)
    kv_index = lambda i, j: ((i * tm) // T // bt, 0, ((i * tm) % T) // tt, 0)
    if aliased:
        kv_block = pl.BlockSpec((None, bt, N_HEADS, tt, HEAD_DIM), lambda i, j: (l,) + kv_index(i, j))
    else:
        kv_block = pl.BlockSpec((depth, bt, N_HEADS, tt, HEAD_DIM), lambda i, j: (0,) + kv_index(i, j))
    kv_shape = jax.ShapeDtypeStruct((depth, B, N_HEADS, T, HEAD_DIM), F32)
    prev_specs = [pl.BlockSpec(memory_space=pl.ANY)] * 2 if aliased else []
    return pl.pallas_call(
        functools.partial(_inproj_kernel, aliased=aliased),
        grid=(n // tm, len(INPROJ_SOURCES)),
        in_specs=[
            pl.BlockSpec((tm, D_MODEL), lambda i, j: (i, 0)),
            pl.BlockSpec((1, D_MODEL), lambda i, j: (0, 0)),
            pl.BlockSpec((None, D_MODEL, INPROJ_TILE), lambda i, j: (j, 0, 0)),
            pl.BlockSpec((D_MODEL, 2 * LANE), lambda i, j: (0, 0)),
        ] + prev_specs,
        out_specs=[
            pl.BlockSpec((tm, INPROJ_TILE), lambda i, j: (i, jnp.minimum(j, SB_TILE - 1))),
            pl.BlockSpec((tm, 2 * LANE), lambda i, j: (i, 0)),
            pl.BlockSpec((tm, W), lambda i, j: (i, 0)),
            kv_block, kv_block,
        ],
        out_shape=[jax.ShapeDtypeStruct((n, N_PROJ), F32), jax.ShapeDtypeStruct((n, 2 * LANE), F32),
                   jax.ShapeDtypeStruct((n, W), BF16), kv_shape, kv_shape],
        input_output_aliases={4: 3, 5: 4} if aliased else {},
        scratch_shapes=[pltpu.VMEM((tm, D_MODEL), BF16)],
        compiler_params=pltpu.CompilerParams(
            dimension_semantics=("parallel", "arbitrary"), vmem_limit_bytes=VMEM_LIMIT),
        name="inproj",
    )(x.reshape(n, D_MODEL), g_pre.reshape(1, D_MODEL), w_tiles, w_if, *(kv_prev if aliased else ()))


def _mlstm_chunk(q_ref, k_ref, v_ref, gi_ref, wc_ref, bias_ref, hm_ref, c_ref, n_ref, m_ref, xs_ref, *, L, rows):
    xs_ref[HALO:HALO + L, 0:BRANCH_WIDTH] = q_ref[rows, :]
    xs_ref[HALO:HALO + L, BRANCH_WIDTH:] = k_ref[rows, :]
    base = HALO - (CONV_WIDTH - 1)
    qk = xs_ref[base:base + L, :] * wc_ref[0:1, :]
    for j in range(1, CONV_WIDTH):
        qk = qk + xs_ref[base + j:base + j + L, :] * wc_ref[j:j + 1, :]
    qk = qk * _sigmoid(qk)
    xs_ref[0:HALO, :] = xs_ref[L:L + HALO, :]

    gates = gi_ref[rows, :] + bias_ref[...]
    ig_all = gates[:, :LANE]
    row = lax.broadcasted_iota(jnp.int32, (L, L), 0)
    col = lax.broadcasted_iota(jnp.int32, (L, L), 1)
    causal = col <= row
    ltri = jnp.where(causal, 1.0, 0.0).astype(BF16)
    bcum = sum(_dot(ltri, piece.astype(BF16)) for piece in _split3(_log_sigmoid(gates[:, LANE:])))
    key_term = jnp.transpose(ig_all - bcum)
    scale = HEAD_DIM ** -0.5

    for h in range(N_HEADS):
        ig = ig_all[:, h:h + 1]
        b = bcum[:, h:h + 1]
        dmat = jnp.where(causal, b + key_term[h:h + 1, :], NEG_BIG)
        m_prev = m_ref[h][:, 0:1]
        inter = b + m_prev
        m_t = jnp.maximum(jnp.max(dmat, axis=-1, keepdims=True), inter)
        qh = qk[:, h * HEAD_DIM:(h + 1) * HEAD_DIM]
        kh = qk[:, BRANCH_WIDTH + h * HEAD_DIM:BRANCH_WIDTH + (h + 1) * HEAD_DIM] * scale
        vh = v_ref[rows, h * HEAD_DIM:(h + 1) * HEAD_DIM]
        qb, kb, vb = qh.astype(BF16), kh.astype(BF16), vh.astype(BF16)
        s = _dot(qb, kb, NT_DIMS) * jnp.exp(dmat - m_t)
        w_inter = jnp.exp(inter - m_t)
        c_prev = c_ref[h]
        n_prev = n_ref[h]
        num = _dot(s.astype(BF16), vb) + w_inter * _dot(qb, c_prev.astype(BF16))
        den = jnp.sum(s, axis=-1, keepdims=True) + w_inter * jnp.sum(qh * n_prev, axis=-1, keepdims=True)
        hm_ref[rows, h * HEAD_DIM:(h + 1) * HEAD_DIM] = num /jnp.maximum(jnp.abs(den), jnp.exp(-m_t))
        m_new = m_t[L - 1:L, :]
        b_last = b[L - 1:L, :]
        dec_s = jnp.exp(b_last - b + ig - m_new)
        dec_c = jnp.exp(b_last + m_prev - m_new)
        kd = kh * dec_s
        c_ref[h] = dec_c * c_prev + _dot(kd.astype(BF16), vb, TN_DIMS)
        n_ref[h] = dec_c * n_prev + jnp.sum(kd, axis=0, keepdims=True)
        m_ref[h] = jnp.broadcast_to(m_new, (1, LANE))


def _hgrn_level_operand(lv, bc, g, kk, q_all, L):
    width = bc.shape[-1]
    rowi = lax.broadcasted_iota(jnp.int32, (L, width), 0)
    first_half = (rowi // lv) % 2 == 0
    if lv == 1:
        e = jnp.where(first_half, 0.0, g)
    elif 2 * lv >= SUBLANES:
        b3 = bc.reshape(L // (2 * lv), 2 * lv, width)
        e = (b3 - b3[:, lv - 1:lv, :]).reshape(L, width)
    else:
        b3 = bc.reshape(L // SUBLANES, SUBLANES, width)
        sub = lax.broadcasted_iota(jnp.int32, b3.shape, 1)
        rho = b3[:, lv - 1:lv, :]
        for start in range(2 * lv, SUBLANES, 2 * lv):
            rho = jnp.where(sub >= start, b3[:, start + lv - 1:start + lv, :], rho)
        e = (b3 - rho).reshape(L, width)
    return (jnp.where(first_half, kk, q_all) * jnp.exp2(-jnp.abs(e))).astype(BF16)


def _hgrn_chunk(q_ref, f_ref, i_ref, la_ref, l1_ref, oml_ref, lvl_ref, o_ref, st_ref, *, L, rows):
    f = f_ref[rows, :]
    la = la_ref[...]
    f = f * LOG2_E
    lb_term = l1_ref[...] + (jnp.minimum(f, 0.0) - _log2_1p_exp2_neg_abs(f))
    g = jnp.maximum(la, lb_term) + _log2_1p_exp2_neg_abs(la - lb_term)
    kk = oml_ref[...] / (1.0 + jnp.exp2(f))
    q_all = q_ref[rows, :]

    row = lax.broadcasted_iota(jnp.int32, (L, L), 0)
    col = lax.broadcasted_iota(jnp.int32, (L, L), 1)
    ltri = jnp.where(col <= row, 1.0, 0.0).astype(BF16)
    bc = sum(_dot(ltri, piece.astype(BF16)) for piece in _split3(g))
    bc_last = bc[L - 1:L, :]
    qe = (q_all * jnp.exp2(bc)).astype(BF16)
    kdec = (kk * jnp.exp2(bc_last - bc)).astype(BF16)
    dec_state = jnp.exp2(bc_last)
    n_levels = L.bit_length() - 1
    operands = [_hgrn_level_operand(1 << p, bc, g, kk, q_all, L) for p in range(n_levels)]
    lvl = lvl_ref[...]

    for h in range(N_HEADS):
        sl = slice(h * HEAD_DIM, (h + 1) * HEAD_DIM)
        ih = i_ref[rows, sl]
        ib = ih.astype(BF16)
        st = st_ref[h]
        a = jnp.zeros((L, L), F32)
        for p, y in enumerate(operands):
            a = jnp.where(lvl == p, _dot(y[:, sl], y[:, sl], NT_DIMS), a)
        diag = jnp.sum(q_all[:, sl] * kk[:, sl], axis=-1, keepdims=True)
        o_ref[rows, sl] = _dot(a.astype(BF16), ib) + _dot(qe[:, sl], st.astype(BF16), NT_DIMS) + diag * ih
        st_ref[h] = st * dec_state[:, sl] + _dot(ib, kdec[:, sl], TN_DIMS)


def _hgrn_level_index(L):
    t = np.arange(L)[:, None]
    s = np.arange(L)[None, :]
    x = np.maximum(t ^ s, 1)
    return np.where(s < t, np.floor(np.log2(x)), -1).astype(np.int32)


def _recurrent_kernel(mq_ref, mk_ref, mv_ref, gi_ref, hq_ref, hf_ref, hi_ref, conv0_ref, wc_ref, bias_ref,
                      c0_ref, n0_ref, m0_ref, la_ref, l1_ref, oml_ref, lvl_ref, s0_ref,
                      hm_ref, c_ref, n_ref, m_ref, oh_ref, st_ref, xs_ref, *, L):
    @pl.when(pl.program_id(1) == 0)
    def _():
        c_ref[...] = c0_ref[...]
        n_ref[...] = n0_ref[...]
        m_ref[...] = m0_ref[...]
        st_ref[...] = s0_ref[...]
        xs_ref[0:HALO, :] = conv0_ref[...]

    for s in range(mq_ref.shape[0] // L):
        rows = slice(s * L, (s + 1) * L)
        _mlstm_chunk(mq_ref, mk_ref, mv_ref, gi_ref, wc_ref, bias_ref, hm_ref, c_ref, n_ref, m_ref, xs_ref,
                     L=L, rows=rows)
        _hgrn_chunk(hq_ref, hf_ref, hi_ref, la_ref, l1_ref, oml_ref, lvl_ref, oh_ref, st_ref, L=L, rows=rows)


def _recurrent(proj, gates, conv0_pad, w_conv, bias_row, c0, n0, m0, la, l1, oml, st0):
    B, T, _ = proj.shape
    L = min(RECURRENT_CHUNK, T)
    rows = min(RECURRENT_BLOCK, T)
    assert L & (L - 1) == 0 and L >= SUBLANES
    wide = lambda blk: pl.BlockSpec((None, rows, BRANCH_WIDTH), lambda b, c: (b, c, blk))
    const = lambda a: pl.BlockSpec(a.shape, lambda b, c: (0,) * a.ndim)
    state = lambda *shp: pl.BlockSpec((None,) + shp, lambda b, c: (b,) + (0,) * len(shp))
    states = [state(N_HEADS, HEAD_DIM, HEAD_DIM), state(N_HEADS, 1, HEAD_DIM), state(N_HEADS, 1, LANE)]
    hgrn_state = state(N_HEADS, HEAD_DIM, HEAD_DIM)
    lvl = jnp.asarray(_hgrn_level_index(L))
    f32 = lambda *shp: jax.ShapeDtypeStruct(shp, F32)
    return pl.pallas_call(
        functools.partial(_recurrent_kernel, L=L),
        grid=(B, T // rows),
        in_specs=[wide(C_MLQ // 4), wide(C_MLK // 4), wide(C_MLV // 4),
                  pl.BlockSpec((None, rows, 2 * LANE), lambda b, c: (b, c, 0)),
                  wide(C_HGQ // 4), wide(C_HGF // 4), wide(C_HGI // 4),
                  state(HALO, D_MODEL), const(w_conv), const(bias_row), *states,
                  const(la), const(l1), const(oml), const(lvl), hgrn_state],
        out_specs=[wide(0), *states, wide(0), hgrn_state],
        out_shape=[f32(B, T, BRANCH_WIDTH), f32(B, N_HEADS, HEAD_DIM, HEAD_DIM), f32(B, N_HEADS, 1, HEAD_DIM),
                   f32(B, N_HEADS, 1, LANE), f32(B, T, BRANCH_WIDTH), f32(B, N_HEADS, HEAD_DIM, HEAD_DIM)],
        scratch_shapes=[pltpu.VMEM((L + HALO, D_MODEL), F32)],
        compiler_params=pltpu.CompilerParams(
            dimension_semantics=("parallel", "arbitrary"), vmem_limit_bytes=VMEM_LIMIT),
        name="recurrent",
    )(proj, proj, proj, gates, proj, proj, proj, conv0_pad, w_conv, bias_row, c0, n0, m0, la, l1, oml, lvl, st0)


SB_KEY_BLOCK = 256
SB_RECENT_KEYS = 512
SB_QUERY_BLOCK = 2048
SB_ZERO_WEIGHT_LOG2 = -150.0


def _strict_upper(n):
    r = lax.broadcasted_iota(jnp.int32, (n, n), 0)
    c = lax.broadcasted_iota(jnp.int32, (n, n), 1)
    u = jnp.where(r > c, 1.0, 0.0).astype(BF16)
    return jnp.concatenate([u, u], axis=0)


def _sb_block(qb, kb, vb, carry, acc, upper, key_offset=None, valid=None):
    z = _dot(qb, kb, NT_DIMS)
    sp = jnp.maximum(z, 0.0) + _log2_1p_exp2_neg_abs(z)
    if valid is not None:
        mask = valid
        lp = jnp.where(mask, -sp, 0.0)
    elif key_offset is None:
        mask = None
        lp = -sp
    else:
        r = lax.broadcasted_iota(jnp.int32, z.shape, 0)
        c = lax.broadcasted_iota(jnp.int32, z.shape, 1)
        mask = c + key_offset < r
        lp = jnp.where(mask, -sp, 0.0)
    lp_hi = lp.astype(BF16)
    lp_lo = (lp - lp_hi.astype(F32)).astype(BF16)
    suffix = _dot(jnp.concatenate([lp_hi, lp_lo], axis=1), upper)
    a = jnp.exp2((z - sp) + suffix + carry)
    if mask is not None:
        a = jnp.where(mask, a, 0.0)
    acc = acc + _dot(a.astype(BF16), vb)
    carry = carry + jnp.sum(lp, axis=-1, keepdims=True)
    return carry, acc


def _sb_walk(qb, carry, acc, n_blocks, load_block, tk):
    upper = _strict_upper(tk)

    def cond(state):
        t, carry, _ = state
        return jnp.logical_and(t < n_blocks, jnp.max(carry) > SB_ZERO_WEIGHT_LOG2)

    def body(state):
        t, carry, acc = state
        kb, vb = load_block(n_blocks - 1 - t)
        carry, acc = _sb_block(qb, kb, vb, carry, acc, upper)
        return t + 1, carry, acc

    _, carry, acc = lax.while_loop(cond, body, (jnp.int32(0), carry, acc))
    return carry, acc


def _sb_prompt_kernel(q_ref, k_ref, v_ref, o_ref, kb_ref, vb_ref, *, tq, tk):
    i = pl.program_id(2)
    T = k_ref.shape[0]
    cast_rows = min(T, 1024)

    @pl.when(i == 0)
    def _():
        def cast(t, _):
            rows = pl.ds(pl.multiple_of(t * cast_rows, cast_rows), cast_rows)
            kb_ref[rows, :] = k_ref[rows, :].astype(BF16)
            vb_ref[rows, :] = v_ref[rows, :].astype(BF16)
            return 0
        lax.fori_loop(0, T // cast_rows, cast, 0)

    def load_block(j):
        rows = pl.ds(pl.multiple_of(j * tk, tk), tk)
        return kb_ref[rows, :], vb_ref[rows, :]

    upper = _strict_upper(tk)
    groups = []
    for s in range(tq // tk):
        blk = i * (tq // tk) + s
        qb = q_ref[s * tk:(s + 1) * tk, :]
        carry, acc = _sb_block(qb, *load_block(blk), jnp.zeros((tk, 1), F32), jnp.zeros((tk, HEAD_DIM), F32), upper,
                               key_offset=0)
        carry, acc = _sb_block(qb, *load_block(jnp.maximum(blk - 1, 0)), carry, acc, upper, valid=blk >= 1)
        groups.append((qb, carry, acc, jnp.maximum(blk - 1, 0)))
    for s, (qb, carry, acc, n_before) in enumerate(groups):
        o_ref[s * tk:(s + 1) * tk, :] = _sb_walk(qb, carry, acc, n_before, load_block, tk)[1]


def _sb_sample_kernel(q_ref, kc_ref, vc_ref, kr_ref, vr_ref, kp_hbm, vp_hbm, o_ref, kbuf_ref, vbuf_ref, sem, *, l, tk):
    tq = q_ref.shape[0]
    b, h = pl.program_id(0), pl.program_id(1)
    qb = q_ref[...]
    carry, acc = _sb_block(qb, kc_ref[...].astype(BF16), vc_ref[...].astype(BF16), jnp.zeros((tq, 1), F32),
                           jnp.zeros((tq, HEAD_DIM), F32), _strict_upper(tq), key_offset=0)

    def load_recent(j):
        rows = pl.ds(pl.multiple_of(j * tk, tk), tk)
        return kr_ref[rows, :].astype(BF16), vr_ref[rows, :].astype(BF16)

    carry, acc = _sb_walk(qb, carry, acc, kr_ref.shape[0] // tk, load_recent, tk)

    def load_older(j):
        rows = pl.ds(pl.multiple_of(j * tk, tk), tk)
        copies = [pltpu.make_async_copy(src.at[l, b, h, rows, :], dst, sem.at[slot])
                  for slot, (src, dst) in enumerate(((kp_hbm, kbuf_ref), (vp_hbm, vbuf_ref)))]
        for cp in copies:
            cp.start()
        for cp in copies:
            cp.wait()
        return kbuf_ref[...].astype(BF16), vbuf_ref[...].astype(BF16)

    n_older = (kp_hbm.shape[3] - kr_ref.shape[0]) // tk
    o_ref[...] = _sb_walk(qb, carry, acc, n_older, load_older, tk)[1]


def _sb_attention(l, q_scaled, k_rows, v_rows, k_past, v_past):
    B, T, _ = q_scaled.shape
    per_head = lambda rows: pl.BlockSpec((None, None, None, rows, HEAD_DIM), lambda b, h, i: (l, b, h, 0, 0))
    if k_past is None:
        tq = min(T, SB_QUERY_BLOCK)
        kernel_fn = functools.partial(_sb_prompt_kernel, tq=tq, tk=min(tq, SB_KEY_BLOCK))
        past_specs, past_args = [], []
        scratch = [pltpu.VMEM((T, HEAD_DIM), BF16), pltpu.VMEM((T, HEAD_DIM), BF16)]
    else:
        tq = T
        P = k_past.shape[3]
        tk, recent = min(P, SB_KEY_BLOCK), min(P, SB_RECENT_KEYS)
        kernel_fn = functools.partial(_sb_sample_kernel, l=l, tk=tk)
        newest = pl.BlockSpec((None, None, None, recent, HEAD_DIM), lambda b, h, i: (l, b, h, P // recent - 1, 0))
        in_hbm = pl.BlockSpec(memory_space=pl.ANY)
        past_specs, past_args = [newest, newest, in_hbm, in_hbm], [k_past, v_past, k_past, v_past]
        scratch = [pltpu.VMEM((tk, HEAD_DIM), F32), pltpu.VMEM((tk, HEAD_DIM), F32), pltpu.SemaphoreType.DMA((2,))]
    return pl.pallas_call(
        kernel_fn,
        grid=(B, N_HEADS, T // tq),
        in_specs=[pl.BlockSpec((None, tq, HEAD_DIM), lambda b, h, i: (b, i, h)), per_head(T), per_head(T)]
        + past_specs,
        out_specs=pl.BlockSpec((None, tq, HEAD_DIM), lambda b, h, i: (b, i, h)),
        out_shape=jax.ShapeDtypeStruct((B, T, BRANCH_WIDTH), F32),
        scratch_shapes=scratch,
        compiler_params=pltpu.CompilerParams(
            dimension_semantics=("parallel", "parallel", "arbitrary"), vmem_limit_bytes=VMEM_LIMIT),
        name="stickbreak",
    )(q_scaled, k_rows, v_rows, *past_args)


def _merge_kernel(hm_ref, os_ref, oh_ref, x_ref, p_ref, gpre_ref, wg0_ref, wg1_ref, wg2_ref, gmlh_ref, ghgh_ref,
                  wb_ref, wo_ref, gpost_ref, wpg_ref, wpp_ref, r_ref):
    wg_refs = (wg0_ref, wg1_ref, wg2_ref)

    def head_norm(a):
        parts = []
        for h in range(N_HEADS):
            seg = a[:, h * HEAD_DIM:(h + 1) * HEAD_DIM]
            parts.append(seg * lax.rsqrt(jnp.mean(seg * seg, axis=-1, keepdims=True) + RMS_EPS))
        return jnp.concatenate(parts, axis=-1)

    x = x_ref[...]
    u = _rms_norm_bf16(x, gpre_ref[...])
    def proj(where, width, skip=0):
        run, start = where
        return _dot(u, wg_refs[run][:, start + skip:start + skip + width])

    silu = lambda a: a * _sigmoid(a)
    h_ml = head_norm(hm_ref[...]) * gmlh_ref[...] * _sigmoid(proj(G_MLO, W)) * silu(proj(G_MLZ, W))
    h_sb = os_ref[...] * silu(proj(G_SBZ, W))
    h_hg = head_norm(oh_ref[...]) * ghgh_ref[...] * silu(proj(G_HGZ, W))
    merged = (_sigmoid(proj(G_GATES, D_MODEL)) * _dot(h_ml.astype(BF16), wb_ref[0])
              + _sigmoid(proj(G_GATES, D_MODEL, D_MODEL)) * _dot(h_sb.astype(BF16), wb_ref[1])
              + _sigmoid(proj(G_GATES, D_MODEL, 2 * D_MODEL)) * _dot(h_hg.astype(BF16), wb_ref[2]))
    y = _dot(merged.astype(BF16), wo_ref[...])
    r = x + y * lax.rsqrt(jnp.mean(y * y, axis=-1, keepdims=True) + RMS_EPS) * gpost_ref[...]
    gate = _sigmoid(_dot(r.astype(BF16), wpg_ref[...]))
    r_ref[...] = r + gate * _dot(p_ref[...].astype(BF16), wpp_ref[...])


def _merge(l, hm, o_s, o_h, x2d, p_all, lw):
    n = x2d.shape[0]
    tm = min(n, MERGE_ROWS)
    rows = lambda width: pl.BlockSpec((tm, width), lambda i: (i, 0))
    full = lambda a: pl.BlockSpec(a.shape, lambda i: (0,) * a.ndim, pipeline_mode=pl.Buffered(1))
    params = [lw["g_pre"], *lw["w_gate"]] + [lw[k] for k in ("g_mlh", "g_hgh", "w_branch", "w_out", "g_post", "w_pg",
                                                              "w_pp")]
    return pl.pallas_call(
        _merge_kernel,
        grid=(n // tm,),
        in_specs=[rows(W), rows(W), rows(W), rows(D_MODEL), pl.BlockSpec((None, tm, PLE_DIM), lambda i: (l, i, 0))]
        + [full(a) for a in params],
        out_specs=rows(D_MODEL),
        out_shape=jax.ShapeDtypeStruct((n, D_MODEL), F32),
        compiler_params=pltpu.CompilerParams(
            dimension_semantics=("parallel",), vmem_limit_bytes=VMEM_LIMIT),
        name="merge",
    )(hm, o_s, o_h, x2d, p_all, *params)


def _layer(l, depth, x, p_all, c0, n0, m0, conv0, s0, k_past, v_past, kv_prev, lw):
    B, T, _ = x.shape
    proj2d, gates2d, sbq2d, k_rows, v_rows = _inproj(l, depth, x, lw["g_pre"], lw["w_in"], lw["w_if"], kv_prev)
    proj = proj2d.reshape(B, T, N_PROJ)

    conv0_pad = jnp.pad(conv0, ((0, 0), (HALO - (CONV_WIDTH - 1), 0), (0, 0)))
    hm, c_new, n_new, m_new, o_h, st_new = _recurrent(
        proj, gates2d.reshape(B, T, 2 * LANE), conv0_pad, lw["w_conv"], lw["bias_row"], c0,
        n0.reshape(B, N_HEADS, 1, HEAD_DIM),
        jnp.broadcast_to(m0[:, :, None, None], (B, N_HEADS, 1, LANE)),
        lw["log_lb"], lw["log1m_lb"], lw["one_m_lb"], jnp.swapaxes(s0, -1, -2))
    o_s = _sb_attention(l, sbq2d.reshape(B, T, BRANCH_WIDTH), k_rows, v_rows, k_past, v_past)

    flat = lambda a: a.reshape(B * T, a.shape[-1])
    r = _merge(l, flat(hm), flat(o_s), flat(o_h), flat(x), p_all.reshape(depth, B * T, PLE_DIM), lw)

    conv_new = proj[:, T - (CONV_WIDTH - 1):, C_MLQ * LANE:C_MLQ * LANE + 2 * BRANCH_WIDTH]
    states = (c_new, n_new[:, :, 0, :], m_new[:, :, 0, 0], conv_new, jnp.swapaxes(st_new, -1, -2))
    return r.reshape(B, T, D_MODEL), states, (k_rows, v_rows)


def _layer_weights(l, lower_bounds, g_pre, w_in, b_i, b_f, w_conv, g_mlh, g_hgh, w_branch, w_out, g_post, w_pg, w_pp):
    cols = lambda span: w_in[l, :, span[0]:span[1]].astype(BF16)
    lane_pad = lambda a: jnp.pad(a, ((0, 0), (0, LANE - a.shape[1])))
    w_if = jnp.concatenate([lane_pad(cols(SRC_ML_I)), lane_pad(cols(SRC_ML_F))], axis=1)
    w_re = jnp.stack([cols(span) for span in INPROJ_SOURCES])
    w_gate = [cols(span) for span in MERGE_SOURCES]
    bias_row = jnp.concatenate([jnp.pad(b_i[l], (0, LANE - N_HEADS)), jnp.pad(b_f[l], (0, LANE - N_HEADS))])
    bias_row = bias_row.reshape(1, 2 * LANE)
    lb = lower_bounds[l].reshape(1, BRANCH_WIDTH)
    return dict(
        g_pre=g_pre[l].reshape(1, D_MODEL), w_in=w_re, w_if=w_if, w_gate=w_gate, bias_row=bias_row,
        w_conv=w_conv[l],
        g_mlh=g_mlh[l].reshape(1, BRANCH_WIDTH), g_hgh=g_hgh[l].reshape(1, BRANCH_WIDTH),
        log_lb=jnp.log(lb) * LOG2_E, log1m_lb=jnp.log1p(-lb) * LOG2_E, one_m_lb=1.0 - lb,
        w_branch=w_branch[l].astype(BF16), w_out=w_out[l].astype(BF16), g_post=g_post[l].reshape(1, D_MODEL),
        w_pg=w_pg[l].astype(BF16), w_pp=w_pp[l].astype(BF16))


def kernel(x_prompt, x_sample, p_prompt, p_sample, state_mlstm_C, state_mlstm_n, state_mlstm_m, state_mlstm_conv,
           cache_sb_k, cache_sb_v, state_hgrn_S, g_pre, w_in, b_mlstm_i, b_mlstm_f, w_mlstm_conv, g_mlstm_head,
           hgrn_lb_logits, g_hgrn_head, w_branch, w_out, g_post, w_ple_gate, w_ple_proj):
    depth = w_in.shape[0]
    lb_cum = jnp.cumsum(jax.nn.softmax(hgrn_lb_logits.astype(F32), axis=0), axis=0)
    lower_bounds = lb_cum - lb_cum[:1]
    B = x_prompt.shape[0]
    yp, ys = x_prompt, x_sample
    new_p, new_s = [], []
    kv_p = kv_s = None
    for l in range(depth):
        lw = _layer_weights(l, lower_bounds, g_pre, w_in, b_mlstm_i, b_mlstm_f, w_mlstm_conv, g_mlstm_head,
                            g_hgrn_head, w_branch, w_out, g_post, w_ple_gate, w_ple_proj)
        zeros = lambda *shape: jnp.zeros(shape, F32)
        yp, st, kv_p = _layer(l, depth, yp, p_prompt, zeros(B, N_HEADS, HEAD_DIM, HEAD_DIM),
                              zeros(B, N_HEADS, HEAD_DIM), zeros(B, N_HEADS),
                              zeros(B, CONV_WIDTH - 1, 2 * BRANCH_WIDTH), zeros(B, N_HEADS, HEAD_DIM, HEAD_DIM),
                              None, None, kv_p, lw)
        new_p.append(st)
        ys, st, kv_s = _layer(l, depth, ys, p_sample, state_mlstm_C[l], state_mlstm_n[l], state_mlstm_m[l],
                              state_mlstm_conv[l], state_hgrn_S[l], cache_sb_k, cache_sb_v, kv_s, lw)
        new_s.append(st)
    stack = lambda sts, i: jnp.stack([s[i] for s in sts])
    outs = lambda sts, kv: (stack(sts, 0), stack(sts, 1), stack(sts, 2), stack(sts, 3), kv[0], kv[1], stack(sts, 4))
    return (yp, ys, *outs(new_p, kv_p), *outs(new_s, kv_s))
```

```python
import functools

import jax
import jax.numpy as jnp
import numpy as np
from jax import lax
from jax.experimental import pallas as pl
from jax.experimental.pallas import tpu as pltpu

F32 = jnp.float32
BF16 = jnp.bfloat16

D_MODEL = 1024
PLE_DIM = 256
N_HEADS = 4
BRANCH_WIDTH = D_MODEL // 2
HEAD_DIM = BRANCH_WIDTH // N_HEADS
CONV_WIDTH = 4
RMS_EPS = 1e-6
LANE = 128
SUBLANES = 8
HALO = SUBLANES
NEG_BIG = -1e30
LOG2_E = 1.4426950408889634
RECURRENT_CHUNK = 128
RECURRENT_BLOCK = 1024
MERGE_ROWS = 512

W = BRANCH_WIDTH
SRC_ML_QKV = (0, 3 * W)
SRC_ML_OZ = (3 * W, 5 * W)
SRC_ML_I = (5 * W, 5 * W + N_HEADS)
SRC_ML_F = (5 * W + N_HEADS, 5 * W + 2 * N_HEADS)
_S = 5 * W + 2 * N_HEADS
SRC_SB_QKV = (_S, _S + 3 * W)
SRC_SB_Z = (_S + 3 * W, _S + 4 * W)
SRC_HG_QFI = (_S + 4 * W, _S + 7 * W)
SRC_HG_Z_GATES = (_S + 7 * W, _S + 8 * W + 3 * D_MODEL)

INPROJ_TILE = 3 * BRANCH_WIDTH
INPROJ_SOURCES = (SRC_ML_QKV, SRC_HG_QFI, SRC_SB_QKV)
SB_TILE = 2
N_PROJ = SB_TILE * INPROJ_TILE
C_MLQ, C_MLK, C_MLV, C_HGQ, C_HGF, C_HGI = 0, 4, 8, 12, 16, 20
MERGE_SOURCES = (SRC_ML_OZ, SRC_SB_Z, SRC_HG_Z_GATES)
G_MLO, G_MLZ, G_SBZ, G_HGZ, G_GATES = (0, 0), (0, W), (1, 0), (2, 0), (2, W)

VMEM_LIMIT = 56 * 1024 * 1024

NT_DIMS = (((1,), (1,)), ((), ()))
TN_DIMS = (((0,), (0,)), ((), ()))


def _sigmoid(x):
    return 1.0 / (1.0 + jnp.exp(-x))


def _softplus_neg_abs(x):
    return jnp.log(1.0 + jnp.exp(-jnp.abs(x)))


def _log2_1p_exp2_neg_abs(x):
    return jnp.log2(1.0 + jnp.exp2(-jnp.abs(x)))


def _log_sigmoid(x):
    return jnp.minimum(x, 0.0) - _softplus_neg_abs(x)


def _dot(a, b, dims=None):
    if dims is None:
        dims = (((a.ndim - 1,), (0,)), ((), ()))
    return lax.dot_general(a, b, dims, preferred_element_type=F32)


def _split3(x):
    hi = x.astype(BF16).astype(F32)
    r = x - hi
    mid = r.astype(BF16).astype(F32)
    return hi, mid, r - mid


def _rms_norm_bf16(x, g):
    ms = jnp.mean(x * x, axis=-1, keepdims=True)
    return (x * lax.rsqrt(ms + RMS_EPS) * g).astype(BF16)


def _inproj_kernel(*refs, aliased):
    x_ref, g_ref, w_ref, wif_ref = refs[:4]
    o_ref, gi_ref, sbq_ref, k_ref, v_ref, u_ref = refs[-6:]
    j = pl.program_id(1)

    @pl.when(j == 0)
    def _():
        u_ref[...] = _rms_norm_bf16(x_ref[...], g_ref[...])
        gi_ref[...] = jnp.dot(u_ref[...], wif_ref[...], preferred_element_type=F32)

    res = jnp.dot(u_ref[...], w_ref[...], preferred_element_type=F32)

    @pl.when(j < SB_TILE)
    def _():
        o_ref[...] = res

    @pl.when(j == SB_TILE)
    def _():
        sbq_ref[...] = (res[:, :W] * (HEAD_DIM ** -0.5 * LOG2_E)).astype(BF16)
        if aliased:
            k_dst, v_dst = k_ref, v_ref
        else:
            k_dst, v_dst = k_ref.at[0], v_ref.at[0]
            if k_ref.shape[0] > 1:
                k_ref[1:] = jnp.zeros((k_ref.shape[0] - 1,) + k_ref.shape[1:], F32)
                v_ref[1:] = jnp.zeros((v_ref.shape[0] - 1,) + v_ref.shape[1:], F32)
        bt, _, tt, _ = k_dst.shape
        for h in range(N_HEADS):
            k_dst[:, h] = res[:, W + h * HEAD_DIM:W + (h + 1) * HEAD_DIM].reshape(bt, tt, HEAD_DIM)
            v_dst[:, h] = res[:, 2 * W + h * HEAD_DIM:2 * W + (h + 1) * HEAD_DIM].reshape(bt, tt, HEAD_DIM)


def _inproj(l, depth, x, g_pre, w_tiles, w_if, kv_prev):
    B, T, _ = x.shape
    n = B * T
    tm = min(n, 1024)
    bt, tt = max(1, tm // T), min(tm, T)
    aliased = kv_prev is not None
    assert aliased == (l > 0)
    kv_index = lambda i, j: ((i * tm) // T // bt, 0, ((i * tm) % T) // tt, 0)
    if aliased:
        kv_block = pl.BlockSpec((None, bt, N_HEADS, tt, HEAD_DIM), lambda i, j: (l,) + kv_index(i, j))
    else:
        kv_block = pl.BlockSpec((depth, bt, N_HEADS, tt, HEAD_DIM), lambda i, j: (0,) + kv_index(i, j))
    kv_shape = jax.ShapeDtypeStruct((depth, B, N_HEADS, T, HEAD_DIM), F32)
    prev_specs = [pl.BlockSpec(memory_space=pl.ANY)] * 2 if aliased else []
    return pl.pallas_call(
        functools.partial(_inproj_kernel, aliased=aliased),
        grid=(n // tm, len(INPROJ_SOURCES)),
        in_specs=[
            pl.BlockSpec((tm, D_MODEL), lambda i, j: (i, 0)),
            pl.BlockSpec((1, D_MODEL), lambda i, j: (0, 0)),
            pl.BlockSpec((None, D_MODEL, INPROJ_TILE), lambda i, j: (j, 0, 0)),
            pl.BlockSpec((D_MODEL, 2 * LANE), lambda i, j: (0, 0)),
        ] + prev_specs,
        out_specs=[
            pl.BlockSpec((tm, INPROJ_TILE), lambda i, j: (i, jnp.minimum(j, SB_TILE - 1))),
            pl.BlockSpec((tm, 2 * LANE), lambda i, j: (i, 0)),
            pl.BlockSpec((tm, W), lambda i, j: (i, 0)),
            kv_block, kv_block,
        ],
        out_shape=[jax.ShapeDtypeStruct((n, N_PROJ), F32), jax.ShapeDtypeStruct((n, 2 * LANE), F32),
                   jax.ShapeDtypeStruct((n, W), BF16), kv_shape, kv_shape],
        input_output_aliases={4: 3, 5: 4} if aliased else {},
        scratch_shapes=[pltpu.VMEM((tm, D_MODEL), BF16)],
        compiler_params=pltpu.CompilerParams(
            dimension_semantics=("parallel", "arbitrary"), vmem_limit_bytes=VMEM_LIMIT),
        name="inproj",
    )(x.reshape(n, D_MODEL), g_pre.reshape(1, D_MODEL), w_tiles, w_if, *(kv_prev if aliased else ()))


def _mlstm_chunk(q_ref, k_ref, v_ref, gi_ref, wc_ref, bias_ref, hm_ref, c_ref, n_ref, m_ref, xs_ref, *, L, rows):
    xs_ref[HALO:HALO + L, 0:BRANCH_WIDTH] = q_ref[rows, :]
    xs_ref[HALO:HALO + L, BRANCH_WIDTH:] = k_ref[rows, :]
    base = HALO - (CONV_WIDTH - 1)
    qk = xs_ref[base:base + L, :] * wc_ref[0:1, :]
    for j in range(1, CONV_WIDTH):
        qk = qk + xs_ref[base + j:base + j + L, :] * wc_ref[j:j + 1, :]
    qk = qk * _sigmoid(qk)
    xs_ref[0:HALO, :] = xs_ref[L:L + HALO, :]

    gates = gi_ref[rows, :] + bias_ref[...]
    ig_all = gates[:, :LANE]
    row = lax.broadcasted_iota(jnp.int32, (L, L), 0)
    col = lax.broadcasted_iota(jnp.int32, (L, L), 1)
    causal = col <= row
    ltri = jnp.where(causal, 1.0, 0.0).astype(BF16)
    bcum = sum(_dot(ltri, piece.astype(BF16)) for piece in _split3(_log_sigmoid(gates[:, LANE:])))
    key_term = jnp.transpose(ig_all - bcum)
    scale = HEAD_DIM ** -0.5

    for h in range(N_HEADS):
        ig = ig_all[:, h:h + 1]
        b = bcum[:, h:h + 1]
        dmat = jnp.where(causal, b + key_term[h:h + 1, :], NEG_BIG)
        m_prev = m_ref[h][:, 0:1]
        inter = b + m_prev
        m_t = jnp.maximum(jnp.max(dmat, axis=-1, keepdims=True), inter)
        qh = qk[:, h * HEAD_DIM:(h + 1) * HEAD_DIM]
        kh = qk[:, BRANCH_WIDTH + h * HEAD_DIM:BRANCH_WIDTH + (h + 1) * HEAD_DIM] * scale
        vh = v_ref[rows, h * HEAD_DIM:(h + 1) * HEAD_DIM]
        qb, kb, vb = qh.astype(BF16), kh.astype(BF16), vh.astype(BF16)
        s = _dot(qb, kb, NT_DIMS) * jnp.exp(dmat - m_t)
        w_inter = jnp.exp(inter - m_t)
        c_prev = c_ref[h]
        n_prev = n_ref[h]
        num = _dot(s.astype(BF16), vb) + w_inter * _dot(qb, c_prev.astype(BF16))
        den = jnp.sum(s, axis=-1, keepdims=True) + w_inter * jnp.sum(qh * n_prev, axis=-1, keepdims=True)
        hm_ref[rows, h * HEAD_DIM:(h + 1) * HEAD_DIM] = num /jnp.maximum(jnp.abs(den), jnp.exp(-m_t))
        m_new = m_t[L - 1:L, :]
        b_last = b[L - 1:L, :]
        dec_s = jnp.exp(b_last - b + ig - m_new)
        dec_c = jnp.exp(b_last + m_prev - m_new)
        kd = kh * dec_s
        c_ref[h] = dec_c * c_prev + _dot(kd.astype(BF16), vb, TN_DIMS)
        n_ref[h] = dec_c * n_prev + jnp.sum(kd, axis=0, keepdims=True)
        m_ref[h] = jnp.broadcast_to(m_new, (1, LANE))


def _hgrn_level_operand(lv, bc, g, kk, q_all, L):
    width = bc.shape[-1]
    rowi = lax.broadcasted_iota(jnp.int32, (L, width), 0)
    first_half = (rowi // lv) % 2 == 0
    if lv == 1:
        e = jnp.where(first_half, 0.0, g)
    elif 2 * lv >= SUBLANES:
        b3 = bc.reshape(L // (2 * lv), 2 * lv, width)
        e = (b3 - b3[:, lv - 1:lv, :]).reshape(L, width)
    else:
        b3 = bc.reshape(L // SUBLANES, SUBLANES, width)
        sub = lax.broadcasted_iota(jnp.int32, b3.shape, 1)
        rho = b3[:, lv - 1:lv, :]
        for start in range(2 * lv, SUBLANES, 2 * lv):
            rho = jnp.where(sub >= start, b3[:, start + lv - 1:start + lv, :], rho)
        e = (b3 - rho).reshape(L, width)
    return (jnp.where(first_half, kk, q_all) * jnp.exp2(-jnp.abs(e))).astype(BF16)


def _hgrn_chunk(q_ref, f_ref, i_ref, la_ref, l1_ref, oml_ref, lvl_ref, o_ref, st_ref, *, L, rows):
    f = f_ref[rows, :]
    la = la_ref[...]
    f = f * LOG2_E
    lb_term = l1_ref[...] + (jnp.minimum(f, 0.0) - _log2_1p_exp2_neg_abs(f))
    g = jnp.maximum(la, lb_term) + _log2_1p_exp2_neg_abs(la - lb_term)
    kk = oml_ref[...] / (1.0 + jnp.exp2(f))
    q_all = q_ref[rows, :]

    row = lax.broadcasted_iota(jnp.int32, (L, L), 0)
    col = lax.broadcasted_iota(jnp.int32, (L, L), 1)
    ltri = jnp.where(col <= row, 1.0, 0.0).astype(BF16)
    bc = sum(_dot(ltri, piece.astype(BF16)) for piece in _split3(g))
    bc_last = bc[L - 1:L, :]
    qe = (q_all * jnp.exp2(bc)).astype(BF16)
    kdec = (kk * jnp.exp2(bc_last - bc)).astype(BF16)
    dec_state = jnp.exp2(bc_last)
    n_levels = L.bit_length() - 1
    operands = [_hgrn_level_operand(1 << p, bc, g, kk, q_all, L) for p in range(n_levels)]
    lvl = lvl_ref[...]

    for h in range(N_HEADS):
        sl = slice(h * HEAD_DIM, (h + 1) * HEAD_DIM)
        ih = i_ref[rows, sl]
        ib = ih.astype(BF16)
        st = st_ref[h]
        a = jnp.zeros((L, L), F32)
        for p, y in enumerate(operands):
            a = jnp.where(lvl == p, _dot(y[:, sl], y[:, sl], NT_DIMS), a)
        diag = jnp.sum(q_all[:, sl] * kk[:, sl], axis=-1, keepdims=True)
        o_ref[rows, sl] = _dot(a.astype(BF16), ib) + _dot(qe[:, sl], st.astype(BF16), NT_DIMS) + diag * ih
        st_ref[h] = st * dec_state[:, sl] + _dot(ib, kdec[:, sl], TN_DIMS)


def _hgrn_level_index(L):
    t = np.arange(L)[:, None]
    s = np.arange(L)[None, :]
    x = np.maximum(t ^ s, 1)
    return np.where(s < t, np.floor(np.log2(x)), -1).astype(np.int32)


def _recurrent_kernel(mq_ref, mk_ref, mv_ref, gi_ref, hq_ref, hf_ref, hi_ref, conv0_ref, wc_ref, bias_ref,
                      c0_ref, n0_ref, m0_ref, la_ref, l1_ref, oml_ref, lvl_ref, s0_ref,
                      hm_ref, c_ref, n_ref, m_ref, oh_ref, st_ref, xs_ref, *, L):
    @pl.when(pl.program_id(1) == 0)
    def _():
        c_ref[...] = c0_ref[...]
        n_ref[...] = n0_ref[...]
        m_ref[...] = m0_ref[...]
        st_ref[...] = s0_ref[...]
        xs_ref[0:HALO, :] = conv0_ref[...]

    for s in range(mq_ref.shape[0] // L):
        rows = slice(s * L, (s + 1) * L)
        _mlstm_chunk(mq_ref, mk_ref, mv_ref, gi_ref, wc_ref, bias_ref, hm_ref, c_ref, n_ref, m_ref, xs_ref,
                     L=L, rows=rows)
        _hgrn_chunk(hq_ref, hf_ref, hi_ref, la_ref, l1_ref, oml_ref, lvl_ref, oh_ref, st_ref, L=L, rows=rows)


def _recurrent(proj, gates, conv0_pad, w_conv, bias_row, c0, n0, m0, la, l1, oml, st0):
    B, T, _ = proj.shape
    L = min(RECURRENT_CHUNK, T)
    rows = min(RECURRENT_BLOCK, T)
    assert L & (L - 1) == 0 and L >= SUBLANES
    wide = lambda blk: pl.BlockSpec((None, rows, BRANCH_WIDTH), lambda b, c: (b, c, blk))
    const = lambda a: pl.BlockSpec(a.shape, lambda b, c: (0,) * a.ndim)
    state = lambda *shp: pl.BlockSpec((None,) + shp, lambda b, c: (b,) + (0,) * len(shp))
    states = [state(N_HEADS, HEAD_DIM, HEAD_DIM), state(N_HEADS, 1, HEAD_DIM), state(N_HEADS, 1, LANE)]
    hgrn_state = state(N_HEADS, HEAD_DIM, HEAD_DIM)
    lvl = jnp.asarray(_hgrn_level_index(L))
    f32 = lambda *shp: jax.ShapeDtypeStruct(shp, F32)
    return pl.pallas_call(
        functools.partial(_recurrent_kernel, L=L),
        grid=(B, T // rows),
        in_specs=[wide(C_MLQ // 4), wide(C_MLK // 4), wide(C_MLV // 4),
                  pl.BlockSpec((None, rows, 2 * LANE), lambda b, c: (b, c, 0)),
                  wide(C_HGQ // 4), wide(C_HGF // 4), wide(C_HGI // 4),
                  state(HALO, D_MODEL), const(w_conv), const(bias_row), *states,
                  const(la), const(l1), const(oml), const(lvl), hgrn_state],
        out_specs=[wide(0), *states, wide(0), hgrn_state],
        out_shape=[f32(B, T, BRANCH_WIDTH), f32(B, N_HEADS, HEAD_DIM, HEAD_DIM), f32(B, N_HEADS, 1, HEAD_DIM),
                   f32(B, N_HEADS, 1, LANE), f32(B, T, BRANCH_WIDTH), f32(B, N_HEADS, HEAD_DIM, HEAD_DIM)],
        scratch_shapes=[pltpu.VMEM((L + HALO, D_MODEL), F32)],
        compiler_params=pltpu.CompilerParams(
            dimension_semantics=("parallel", "arbitrary"), vmem_limit_bytes=VMEM_LIMIT),
        name="recurrent",
    )(proj, proj, proj, gates, proj, proj, proj, conv0_pad, w_conv, bias_row, c0, n0, m0, la, l1, oml, lvl, st0)


SB_KEY_BLOCK = 256
SB_RECENT_KEYS = 256
SB_QUERY_BLOCK = 2048
SB_ZERO_WEIGHT_LOG2 = -150.0


def _strict_upper(n):
    r = lax.broadcasted_iota(jnp.int32, (n, n), 0)
    c = lax.broadcasted_iota(jnp.int32, (n, n), 1)
    u = jnp.where(r > c, 1.0, 0.0).astype(BF16)
    return jnp.concatenate([u, u], axis=0)


def _sb_block(qb, kb, vb, carry, acc, upper, key_offset=None, valid=None):
    z = _dot(qb, kb, NT_DIMS)
    sp = jnp.maximum(z, 0.0) + _log2_1p_exp2_neg_abs(z)
    if valid is not None:
        mask = valid
        lp = jnp.where(mask, -sp, 0.0)
    elif key_offset is None:
        mask = None
        lp = -sp
    else:
        r = lax.broadcasted_iota(jnp.int32, z.shape, 0)
        c = lax.broadcasted_iota(jnp.int32, z.shape, 1)
        mask = c + key_offset < r
        lp = jnp.where(mask, -sp, 0.0)
    lp_hi = lp.astype(BF16)
    lp_lo = (lp - lp_hi.astype(F32)).astype(BF16)
    suffix = _dot(jnp.concatenate([lp_hi, lp_lo], axis=1), upper)
    a = jnp.exp2((z - sp) + suffix + carry)
    if mask is not None:
        a = jnp.where(mask, a, 0.0)
    acc = acc + _dot(a.astype(BF16), vb)
    carry = carry + jnp.sum(lp, axis=-1, keepdims=True)
    return carry, acc


def _sb_walk(qb, carry, acc, n_blocks, load_block, tk):
    upper = _strict_upper(tk)

    def cond(state):
        t, carry, _ = state
        return jnp.logical_and(t < n_blocks, jnp.max(carry) > SB_ZERO_WEIGHT_LOG2)

    def body(state):
        t, carry, acc = state
        kb, vb = load_block(n_blocks - 1 - t)
        carry, acc = _sb_block(qb, kb, vb, carry, acc, upper)
        return t + 1, carry, acc

    _, carry, acc = lax.while_loop(cond, body, (jnp.int32(0), carry, acc))
    return carry, acc


def _sb_prompt_kernel(q_ref, k_ref, v_ref, o_ref, kb_ref, vb_ref, *, tq, tk):
    i = pl.program_id(2)
    T = k_ref.shape[0]
    cast_rows = min(T, 1024)

    @pl.when(i == 0)
    def _():
        def cast(t, _):
            rows = pl.ds(pl.multiple_of(t * cast_rows, cast_rows), cast_rows)
            kb_ref[rows, :] = k_ref[rows, :].astype(BF16)
            vb_ref[rows, :] = v_ref[rows, :].astype(BF16)
            return 0
        lax.fori_loop(0, T // cast_rows, cast, 0)

    def load_block(j):
        rows = pl.ds(pl.multiple_of(j * tk, tk), tk)
        return kb_ref[rows, :], vb_ref[rows, :]

    upper = _strict_upper(tk)
    groups = []
    for s in range(tq // tk):
        blk = i * (tq // tk) + s
        qb = q_ref[s * tk:(s + 1) * tk, :]
        carry, acc = _sb_block(qb, *load_block(blk), jnp.zeros((tk, 1), F32), jnp.zeros((tk, HEAD_DIM), F32), upper,
                               key_offset=0)
        carry, acc = _sb_block(qb, *load_block(jnp.maximum(blk - 1, 0)), carry, acc, upper, valid=blk >= 1)
        groups.append((qb, carry, acc, jnp.maximum(blk - 1, 0)))
    for s, (qb, carry, acc, n_before) in enumerate(groups):
        o_ref[s * tk:(s + 1) * tk, :] = _sb_walk(qb, carry, acc, n_before, load_block, tk)[1]


def _sb_sample_kernel(q_ref, kc_ref, vc_ref, kr_ref, vr_ref, kp_hbm, vp_hbm, o_ref, kbuf_ref, vbuf_ref, sem, *, l, tk):
    tq = q_ref.shape[0]
    b, h = pl.program_id(0), pl.program_id(1)
    qb = q_ref[...]
    carry, acc = _sb_block(qb, kc_ref[...].astype(BF16), vc_ref[...].astype(BF16), jnp.zeros((tq, 1), F32),
                           jnp.zeros((tq, HEAD_DIM), F32), _strict_upper(tq), key_offset=0)

    def load_recent(j):
        rows = pl.ds(pl.multiple_of(j * tk, tk), tk)
        return kr_ref[rows, :].astype(BF16), vr_ref[rows, :].astype(BF16)

    carry, acc = _sb_walk(qb, carry, acc, kr_ref.shape[0] // tk, load_recent, tk)

    def load_older(j):
        rows = pl.ds(pl.multiple_of(j * tk, tk), tk)
        copies = [pltpu.make_async_copy(src.at[l, b, h, rows, :], dst, sem.at[slot])
                  for slot, (src, dst) in enumerate(((kp_hbm, kbuf_ref), (vp_hbm, vbuf_ref)))]
        for cp in copies:
            cp.start()
        for cp in copies:
            cp.wait()
        return kbuf_ref[...].astype(BF16), vbuf_ref[...].astype(BF16)

    n_older = (kp_hbm.shape[3] - kr_ref.shape[0]) // tk
    o_ref[...] = _sb_walk(qb, carry, acc, n_older, load_older, tk)[1]


def _sb_attention(l, q_scaled, k_rows, v_rows, k_past, v_past):
    B, T, _ = q_scaled.shape
    per_head = lambda rows: pl.BlockSpec((None, None, None, rows, HEAD_DIM), lambda b, h, i: (l, b, h, 0, 0))
    if k_past is None:
        tq = min(T, SB_QUERY_BLOCK)
        kernel_fn = functools.partial(_sb_prompt_kernel, tq=tq, tk=min(tq, SB_KEY_BLOCK))
        past_specs, past_args = [], []
        scratch = [pltpu.VMEM((T, HEAD_DIM), BF16), pltpu.VMEM((T, HEAD_DIM), BF16)]
    else:
        tq = T
        P = k_past.shape[3]
        tk, recent = min(P, SB_KEY_BLOCK), min(P, SB_RECENT_KEYS)
        kernel_fn = functools.partial(_sb_sample_kernel, l=l, tk=tk)
        newest = pl.BlockSpec((None, None, None, recent, HEAD_DIM), lambda b, h, i: (l, b, h, P // recent - 1, 0))
        in_hbm = pl.BlockSpec(memory_space=pl.ANY)
        past_specs, past_args = [newest, newest, in_hbm, in_hbm], [k_past, v_past, k_past, v_past]
        scratch = [pltpu.VMEM((tk, HEAD_DIM), F32), pltpu.VMEM((tk, HEAD_DIM), F32), pltpu.SemaphoreType.DMA((2,))]
    return pl.pallas_call(
        kernel_fn,
        grid=(B, N_HEADS, T // tq),
        in_specs=[pl.BlockSpec((None, tq, HEAD_DIM), lambda b, h, i: (b, i, h)), per_head(T), per_head(T)]
        + past_specs,
        out_specs=pl.BlockSpec((None, tq, HEAD_DIM), lambda b, h, i: (b, i, h)),
        out_shape=jax.ShapeDtypeStruct((B, T, BRANCH_WIDTH), F32),
        scratch_shapes=scratch,
        compiler_params=pltpu.CompilerParams(
            dimension_semantics=("parallel", "parallel", "arbitrary"), vmem_limit_bytes=VMEM_LIMIT),
        name="stickbreak",
    )(q_scaled, k_rows, v_rows, *past_args)


def _merge_kernel(hm_ref, os_ref, oh_ref, x_ref, p_ref, gpre_ref, wg0_ref, wg1_ref, wg2_ref, gmlh_ref, ghgh_ref,
                  wb_ref, wo_ref, gpost_ref, wpg_ref, wpp_ref, r_ref):
    wg_refs = (wg0_ref, wg1_ref, wg2_ref)

    def head_norm(a):
        parts = []
        for h in range(N_HEADS):
            seg = a[:, h * HEAD_DIM:(h + 1) * HEAD_DIM]
            parts.append(seg * lax.rsqrt(jnp.mean(seg * seg, axis=-1, keepdims=True) + RMS_EPS))
        return jnp.concatenate(parts, axis=-1)

    x = x_ref[...]
    u = _rms_norm_bf16(x, gpre_ref[...])
    def proj(where, width, skip=0):
        run, start = where
        return _dot(u, wg_refs[run][:, start + skip:start + skip + width])

    silu = lambda a: a * _sigmoid(a)
    h_ml = head_norm(hm_ref[...]) * gmlh_ref[...] * _sigmoid(proj(G_MLO, W)) * silu(proj(G_MLZ, W))
    h_sb = os_ref[...] * silu(proj(G_SBZ, W))
    h_hg = head_norm(oh_ref[...]) * ghgh_ref[...] * silu(proj(G_HGZ, W))
    merged = (_sigmoid(proj(G_GATES, D_MODEL)) * _dot(h_ml.astype(BF16), wb_ref[0])
              + _sigmoid(proj(G_GATES, D_MODEL, D_MODEL)) * _dot(h_sb.astype(BF16), wb_ref[1])
              + _sigmoid(proj(G_GATES, D_MODEL, 2 * D_MODEL)) * _dot(h_hg.astype(BF16), wb_ref[2]))
    y = _dot(merged.astype(BF16), wo_ref[...])
    r = x + y * lax.rsqrt(jnp.mean(y * y, axis=-1, keepdims=True) + RMS_EPS) * gpost_ref[...]
    gate = _sigmoid(_dot(r.astype(BF16), wpg_ref[...]))
    r_ref[...] = r + gate * _dot(p_ref[...].astype(BF16), wpp_ref[...])


def _merge(l, hm, o_s, o_h, x2d, p_all, lw):
    n = x2d.shape[0]
    tm = min(n, MERGE_ROWS)
    rows = lambda width: pl.BlockSpec((tm, width), lambda i: (i, 0))
    full = lambda a: pl.BlockSpec(a.shape, lambda i: (0,) * a.ndim, pipeline_mode=pl.Buffered(1))
    params = [lw["g_pre"], *lw["w_gate"]] + [lw[k] for k in ("g_mlh", "g_hgh", "w_branch", "w_out", "g_post", "w_pg",
                                                              "w_pp")]
    return pl.pallas_call(
        _merge_kernel,
        grid=(n // tm,),
        in_specs=[rows(W), rows(W), rows(W), rows(D_MODEL), pl.BlockSpec((None, tm, PLE_DIM), lambda i: (l, i, 0))]
        + [full(a) for a in params],
        out_specs=rows(D_MODEL),
        out_shape=jax.ShapeDtypeStruct((n, D_MODEL), F32),
        compiler_params=pltpu.CompilerParams(
            dimension_semantics=("parallel",), vmem_limit_bytes=VMEM_LIMIT),
        name="merge",
    )(hm, o_s, o_h, x2d, p_all, *params)


def _layer(l, depth, x, p_all, c0, n0, m0, conv0, s0, k_past, v_past, kv_prev, lw):
    B, T, _ = x.shape
    proj2d, gates2d, sbq2d, k_rows, v_rows = _inproj(l, depth, x, lw["g_pre"], lw["w_in"], lw["w_if"], kv_prev)
    proj = proj2d.reshape(B, T, N_PROJ)

    conv0_pad = jnp.pad(conv0, ((0, 0), (HALO - (CONV_WIDTH - 1), 0), (0, 0)))
    hm, c_new, n_new, m_new, o_h, st_new = _recurrent(
        proj, gates2d.reshape(B, T, 2 * LANE), conv0_pad, lw["w_conv"], lw["bias_row"], c0,
        n0.reshape(B, N_HEADS, 1, HEAD_DIM),
        jnp.broadcast_to(m0[:, :, None, None], (B, N_HEADS, 1, LANE)),
        lw["log_lb"], lw["log1m_lb"], lw["one_m_lb"], jnp.swapaxes(s0, -1, -2))
    o_s = _sb_attention(l, sbq2d.reshape(B, T, BRANCH_WIDTH), k_rows, v_rows, k_past, v_past)

    flat = lambda a: a.reshape(B * T, a.shape[-1])
    r = _merge(l, flat(hm), flat(o_s), flat(o_h), flat(x), p_all.reshape(depth, B * T, PLE_DIM), lw)

    conv_new = proj[:, T - (CONV_WIDTH - 1):, C_MLQ * LANE:C_MLQ * LANE + 2 * BRANCH_WIDTH]
    states = (c_new, n_new[:, :, 0, :], m_new[:, :, 0, 0], conv_new, jnp.swapaxes(st_new, -1, -2))
    return r.reshape(B, T, D_MODEL), states, (k_rows, v_rows)


def _layer_weights(l, lower_bounds, g_pre, w_in, b_i, b_f, w_conv, g_mlh, g_hgh, w_branch, w_out, g_post, w_pg, w_pp):
    cols = lambda span: w_in[l, :, span[0]:span[1]].astype(BF16)
    lane_pad = lambda a: jnp.pad(a, ((0, 0), (0, LANE - a.shape[1])))
    w_if = jnp.concatenate([lane_pad(cols(SRC_ML_I)), lane_pad(cols(SRC_ML_F))], axis=1)
    w_re = jnp.stack([cols(span) for span in INPROJ_SOURCES])
    w_gate = [cols(span) for span in MERGE_SOURCES]
    bias_row = jnp.concatenate([jnp.pad(b_i[l], (0, LANE - N_HEADS)), jnp.pad(b_f[l], (0, LANE - N_HEADS))])
    bias_row = bias_row.reshape(1, 2 * LANE)
    lb = lower_bounds[l].reshape(1, BRANCH_WIDTH)
    return dict(
        g_pre=g_pre[l].reshape(1, D_MODEL), w_in=w_re, w_if=w_if, w_gate=w_gate, bias_row=bias_row,
        w_conv=w_conv[l],
        g_mlh=g_mlh[l].reshape(1, BRANCH_WIDTH), g_hgh=g_hgh[l].reshape(1, BRANCH_WIDTH),
        log_lb=jnp.log(lb) * LOG2_E, log1m_lb=jnp.log1p(-lb) * LOG2_E, one_m_lb=1.0 - lb,
        w_branch=w_branch[l].astype(BF16), w_out=w_out[l].astype(BF16), g_post=g_post[l].reshape(1, D_MODEL),
        w_pg=w_pg[l].astype(BF16), w_pp=w_pp[l].astype(BF16))


def kernel(x_prompt, x_sample, p_prompt, p_sample, state_mlstm_C, state_mlstm_n, state_mlstm_m, state_mlstm_conv,
           cache_sb_k, cache_sb_v, state_hgrn_S, g_pre, w_in, b_mlstm_i, b_mlstm_f, w_mlstm_conv, g_mlstm_head,
           hgrn_lb_logits, g_hgrn_head, w_branch, w_out, g_post, w_ple_gate, w_ple_proj):
    depth = w_in.shape[0]
    lb_cum = jnp.cumsum(jax.nn.softmax(hgrn_lb_logits.astype(F32), axis=0), axis=0)
    lower_bounds = lb_cum - lb_cum[:1]
    B = x_prompt.shape[0]
    yp, ys = x_prompt, x_sample
    new_p, new_s = [], []
    kv_p = kv_s = None
    for l in range(depth):
        lw = _layer_weights(l, lower_bounds, g_pre, w_in, b_mlstm_i, b_mlstm_f, w_mlstm_conv, g_mlstm_head,
                            g_hgrn_head, w_branch, w_out, g_post, w_ple_gate, w_ple_proj)
        zeros = lambda *shape: jnp.zeros(shape, F32)
        yp, st, kv_p = _layer(l, depth, yp, p_prompt, zeros(B, N_HEADS, HEAD_DIM, HEAD_DIM),
                              zeros(B, N_HEADS, HEAD_DIM), zeros(B, N_HEADS),
                              zeros(B, CONV_WIDTH - 1, 2 * BRANCH_WIDTH), zeros(B, N_HEADS, HEAD_DIM, HEAD_DIM),
                              None, None, kv_p, lw)
        new_p.append(st)
        ys, st, kv_s = _layer(l, depth, ys, p_sample, state_mlstm_C[l], state_mlstm_n[l], state_mlstm_m[l],
                              state_mlstm_conv[l], state_hgrn_S[l], cache_sb_k, cache_sb_v, kv_s, lw)
        new_s.append(st)
    stack = lambda sts, i: jnp.stack([s[i] for s in sts])
    outs = lambda sts, kv: (stack(sts, 0), stack(sts, 1), stack(sts, 2), stack(sts, 3), kv[0], kv[1], stack(sts, 4))
    return (yp, ys, *outs(new_p, kv_p), *outs(new_s, kv_s))
```

```python
import functools

import jax
import jax.numpy as jnp
import numpy as np
from jax import lax
from jax.experimental import pallas as pl
from jax.experimental.pallas import tpu as pltpu

F32 = jnp.float32
BF16 = jnp.bfloat16

D_MODEL = 1024
PLE_DIM = 256
N_HEADS = 4
BRANCH_WIDTH = D_MODEL // 2
HEAD_DIM = BRANCH_WIDTH // N_HEADS
CONV_WIDTH = 4
RMS_EPS = 1e-6
LANE = 128
SUBLANES = 8
HALO = SUBLANES
NEG_BIG = -1e30
LOG2_E = 1.4426950408889634
RECURRENT_CHUNK = 128
RECURRENT_BLOCK = 1024
MERGE_ROWS = 512

W = BRANCH_WIDTH
SRC_ML_QKV = (0, 3 * W)
SRC_ML_OZ = (3 * W, 5 * W)
SRC_ML_I = (5 * W, 5 * W + N_HEADS)
SRC_ML_F = (5 * W + N_HEADS, 5 * W + 2 * N_HEADS)
_S = 5 * W + 2 * N_HEADS
SRC_SB_QKV = (_S, _S + 3 * W)
SRC_SB_Z = (_S + 3 * W, _S + 4 * W)
SRC_HG_QFI = (_S + 4 * W, _S + 7 * W)
SRC_HG_Z_GATES = (_S + 7 * W, _S + 8 * W + 3 * D_MODEL)

INPROJ_TILE = 3 * BRANCH_WIDTH
INPROJ_SOURCES = (SRC_ML_QKV, SRC_HG_QFI, SRC_SB_QKV)
SB_TILE = 2
N_PROJ = SB_TILE * INPROJ_TILE
C_MLQ, C_MLK, C_MLV, C_HGQ, C_HGF, C_HGI = 0, 4, 8, 12, 16, 20
MERGE_SOURCES = (SRC_ML_OZ, SRC_SB_Z, SRC_HG_Z_GATES)
G_MLO, G_MLZ, G_SBZ, G_HGZ, G_GATES = (0, 0), (0, W), (1, 0), (2, 0), (2, W)

VMEM_LIMIT = 56 * 1024 * 1024

NT_DIMS = (((1,), (1,)), ((), ()))
TN_DIMS = (((0,), (0,)), ((), ()))


def _sigmoid(x):
    return 1.0 / (1.0 + jnp.exp(-x))


def _softplus_neg_abs(x):
    return jnp.log(1.0 + jnp.exp(-jnp.abs(x)))


def _log2_1p_exp2_neg_abs(x):
    return jnp.log2(1.0 + jnp.exp2(-jnp.abs(x)))


def _log_sigmoid(x):
    return jnp.minimum(x, 0.0) - _softplus_neg_abs(x)


def _dot(a, b, dims=None):
    if dims is None:
        dims = (((a.ndim - 1,), (0,)), ((), ()))
    return lax.dot_general(a, b, dims, preferred_element_type=F32)


def _split3(x):
    hi = x.astype(BF16).astype(F32)
    r = x - hi
    mid = r.astype(BF16).astype(F32)
    return hi, mid, r - mid


def _rms_norm_bf16(x, g):
    ms = jnp.mean(x * x, axis=-1, keepdims=True)
    return (x * lax.rsqrt(ms + RMS_EPS) * g).astype(BF16)


def _inproj_kernel(*refs, aliased):
    x_ref, g_ref, w_ref, wif_ref = refs[:4]
    o_ref, gi_ref, sbq_ref, k_ref, v_ref, u_ref = refs[-6:]
    j = pl.program_id(1)

    @pl.when(j == 0)
    def _():
        u_ref[...] = _rms_norm_bf16(x_ref[...], g_ref[...])
        gi_ref[...] = jnp.dot(u_ref[...], wif_ref[...], preferred_element_type=F32)

    res = jnp.dot(u_ref[...], w_ref[...], preferred_element_type=F32)

    @pl.when(j < SB_TILE)
    def _():
        o_ref[...] = res

    @pl.when(j == SB_TILE)
    def _():
        sbq_ref[...] = (res[:, :W] * (HEAD_DIM ** -0.5 * LOG2_E)).astype(BF16)
        if aliased:
            k_dst, v_dst = k_ref, v_ref
        else:
            k_dst, v_dst = k_ref.at[0], v_ref.at[0]
            if k_ref.shape[0] > 1:
                k_ref[1:] = jnp.zeros((k_ref.shape[0] - 1,) + k_ref.shape[1:], F32)
                v_ref[1:] = jnp.zeros((v_ref.shape[0] - 1,) + v_ref.shape[1:], F32)
        bt, _, tt, _ = k_dst.shape
        for h in range(N_HEADS):
            k_dst[:, h] = res[:, W + h * HEAD_DIM:W + (h + 1) * HEAD_DIM].reshape(bt, tt, HEAD_DIM)
            v_dst[:, h] = res[:, 2 * W + h * HEAD_DIM:2 * W + (h + 1) * HEAD_DIM].reshape(bt, tt, HEAD_DIM)


def _inproj(l, depth, x, g_pre, w_tiles, w_if, kv_prev):
    B, T, _ = x.shape
    n = B * T
    tm = min(n, 1024)
    bt, tt = max(1, tm // T), min(tm, T)
    aliased = kv_prev is not None
    assert aliased == (l > 0)
    kv_index = lambda i, j: ((i * tm) // T // bt, 0, ((i * tm) % T) // tt, 0)
    if aliased:
        kv_block = pl.BlockSpec((None, bt, N_HEADS, tt, HEAD_DIM), lambda i, j: (l,) + kv_index(i, j))
    else:
        kv_block = pl.BlockSpec((depth, bt, N_HEADS, tt, HEAD_DIM), lambda i, j: (0,) + kv_index(i, j))
    kv_shape = jax.ShapeDtypeStruct((depth, B, N_HEADS, T, HEAD_DIM), F32)
    prev_specs = [pl.BlockSpec(memory_space=pl.ANY)] * 2 if aliased else []
    return pl.pallas_call(
        functools.partial(_inproj_kernel, aliased=aliased),
        grid=(n // tm, len(INPROJ_SOURCES)),
        in_specs=[
            pl.BlockSpec((tm, D_MODEL), lambda i, j: (i, 0)),
            pl.BlockSpec((1, D_MODEL), lambda i, j: (0, 0)),
            pl.BlockSpec((None, D_MODEL, INPROJ_TILE), lambda i, j: (j, 0, 0)),
            pl.BlockSpec((D_MODEL, 2 * LANE), lambda i, j: (0, 0)),
        ] + prev_specs,
        out_specs=[
            pl.BlockSpec((tm, INPROJ_TILE), lambda i, j: (i, jnp.minimum(j, SB_TILE - 1))),
            pl.BlockSpec((tm, 2 * LANE), lambda i, j: (i, 0)),
            pl.BlockSpec((tm, W), lambda i, j: (i, 0)),
            kv_block, kv_block,
        ],
        out_shape=[jax.ShapeDtypeStruct((n, N_PROJ), F32), jax.ShapeDtypeStruct((n, 2 * LANE), F32),
                   jax.ShapeDtypeStruct((n, W), BF16), kv_shape, kv_shape],
        input_output_aliases={4: 3, 5: 4} if aliased else {},
        scratch_shapes=[pltpu.VMEM((tm, D_MODEL), BF16)],
        compiler_params=pltpu.CompilerParams(
            dimension_semantics=("parallel", "arbitrary"), vmem_limit_bytes=VMEM_LIMIT),
        name="inproj",
    )(x.reshape(n, D_MODEL), g_pre.reshape(1, D_MODEL), w_tiles, w_if, *(kv_prev if aliased else ()))


def _mlstm_chunk(q_ref, k_ref, v_ref, gi_ref, wc_ref, bias_ref, hm_ref, c_ref, n_ref, m_ref, xs_ref, *, L, rows):
    xs_ref[HALO:HALO + L, 0:BRANCH_WIDTH] = q_ref[rows, :]
    xs_ref[HALO:HALO + L, BRANCH_WIDTH:] = k_ref[rows, :]
    base = HALO - (CONV_WIDTH - 1)
    qk = xs_ref[base:base + L, :] * wc_ref[0:1, :]
    for j in range(1, CONV_WIDTH):
        qk = qk + xs_ref[base + j:base + j + L, :] * wc_ref[j:j + 1, :]
    qk = qk * _sigmoid(qk)
    xs_ref[0:HALO, :] = xs_ref[L:L + HALO, :]

    gates = gi_ref[rows, :] + bias_ref[...]
    ig_all = gates[:, :LANE]
    row = lax.broadcasted_iota(jnp.int32, (L, L), 0)
    col = lax.broadcasted_iota(jnp.int32, (L, L), 1)
    causal = col <= row
    ltri = jnp.where(causal, 1.0, 0.0).astype(BF16)
    bcum = sum(_dot(ltri, piece.astype(BF16)) for piece in _split3(_log_sigmoid(gates[:, LANE:])))
    key_term = jnp.transpose(ig_all - bcum)
    scale = HEAD_DIM ** -0.5

    for h in range(N_HEADS):
        ig = ig_all[:, h:h + 1]
        b = bcum[:, h:h + 1]
        dmat = jnp.where(causal, b + key_term[h:h + 1, :], NEG_BIG)
        m_prev = m_ref[h][:, 0:1]
        inter = b + m_prev
        m_t = jnp.maximum(jnp.max(dmat, axis=-1, keepdims=True), inter)
        qh = qk[:, h * HEAD_DIM:(h + 1) * HEAD_DIM]
        kh = qk[:, BRANCH_WIDTH + h * HEAD_DIM:BRANCH_WIDTH + (h + 1) * HEAD_DIM] * scale
        vh = v_ref[rows, h * HEAD_DIM:(h + 1) * HEAD_DIM]
        qb, kb, vb = qh.astype(BF16), kh.astype(BF16), vh.astype(BF16)
        s = _dot(qb, kb, NT_DIMS) * jnp.exp(dmat - m_t)
        w_inter = jnp.exp(inter - m_t)
        c_prev = c_ref[h]
        n_prev = n_ref[h]
        num = _dot(s.astype(BF16), vb) + w_inter * _dot(qb, c_prev.astype(BF16))
        den = jnp.sum(s, axis=-1, keepdims=True) + w_inter * jnp.sum(qh * n_prev, axis=-1, keepdims=True)
        hm_ref[rows, h * HEAD_DIM:(h + 1) * HEAD_DIM] = num /jnp.maximum(jnp.abs(den), jnp.exp(-m_t))
        m_new = m_t[L - 1:L, :]
        b_last = b[L - 1:L, :]
        dec_s = jnp.exp(b_last - b + ig - m_new)
        dec_c = jnp.exp(b_last + m_prev - m_new)
        kd = kh * dec_s
        c_ref[h] = dec_c * c_prev + _dot(kd.astype(BF16), vb, TN_DIMS)
        n_ref[h] = dec_c * n_prev + jnp.sum(kd, axis=0, keepdims=True)
        m_ref[h] = jnp.broadcast_to(m_new, (1, LANE))


def _hgrn_level_operand(lv, bc, g, kk, q_all, L):
    width = bc.shape[-1]
    rowi = lax.broadcasted_iota(jnp.int32, (L, width), 0)
    first_half = (rowi // lv) % 2 == 0
    if lv == 1:
        e = jnp.where(first_half, 0.0, g)
    elif 2 * lv >= SUBLANES:
        b3 = bc.reshape(L // (2 * lv), 2 * lv, width)
        e = (b3 - b3[:, lv - 1:lv, :]).reshape(L, width)
    else:
        b3 = bc.reshape(L // SUBLANES, SUBLANES, width)
        sub = lax.broadcasted_iota(jnp.int32, b3.shape, 1)
        rho = b3[:, lv - 1:lv, :]
        for start in range(2 * lv, SUBLANES, 2 * lv):
            rho = jnp.where(sub >= start, b3[:, start + lv - 1:start + lv, :], rho)
        e = (b3 - rho).reshape(L, width)
    return (jnp.where(first_half, kk, q_all) * jnp.exp2(-jnp.abs(e))).astype(BF16)


def _hgrn_chunk(q_ref, f_ref, i_ref, la_ref, l1_ref, oml_ref, lvl_ref, o_ref, st_ref, *, L, rows):
    f = f_ref[rows, :]
    la = la_ref[...]
    f = f * LOG2_E
    lb_term = l1_ref[...] + (jnp.minimum(f, 0.0) - _log2_1p_exp2_neg_abs(f))
    g = jnp.maximum(la, lb_term) + _log2_1p_exp2_neg_abs(la - lb_term)
    kk = oml_ref[...] / (1.0 + jnp.exp2(f))
    q_all = q_ref[rows, :]

    row = lax.broadcasted_iota(jnp.int32, (L, L), 0)
    col = lax.broadcasted_iota(jnp.int32, (L, L), 1)
    ltri = jnp.where(col <= row, 1.0, 0.0).astype(BF16)
    bc = sum(_dot(ltri, piece.astype(BF16)) for piece in _split3(g))
    bc_last = bc[L - 1:L, :]
    qe = (q_all * jnp.exp2(bc)).astype(BF16)
    kdec = (kk * jnp.exp2(bc_last - bc)).astype(BF16)
    dec_state = jnp.exp2(bc_last)
    n_levels = L.bit_length() - 1
    operands = [_hgrn_level_operand(1 << p, bc, g, kk, q_all, L) for p in range(n_levels)]
    lvl = lvl_ref[...]

    for h in range(N_HEADS):
        sl = slice(h * HEAD_DIM, (h + 1) * HEAD_DIM)
        ih = i_ref[rows, sl]
        ib = ih.astype(BF16)
        st = st_ref[h]
        a = jnp.zeros((L, L), F32)
        for p, y in enumerate(operands):
            a = jnp.where(lvl == p, _dot(y[:, sl], y[:, sl], NT_DIMS), a)
        diag = jnp.sum(q_all[:, sl] * kk[:, sl], axis=-1, keepdims=True)
        o_ref[rows, sl] = _dot(a.astype(BF16), ib) + _dot(qe[:, sl], st.astype(BF16), NT_DIMS) + diag * ih
        st_ref[h] = st * dec_state[:, sl] + _dot(ib, kdec[:, sl], TN_DIMS)


def _hgrn_level_index(L):
    t = np.arange(L)[:, None]
    s = np.arange(L)[None, :]
    x = np.maximum(t ^ s, 1)
    return np.where(s < t, np.floor(np.log2(x)), -1).astype(np.int32)


def _recurrent_kernel(mq_ref, mk_ref, mv_ref, gi_ref, hq_ref, hf_ref, hi_ref, conv0_ref, wc_ref, bias_ref,
                      c0_ref, n0_ref, m0_ref, la_ref, l1_ref, oml_ref, lvl_ref, s0_ref,
                      hm_ref, c_ref, n_ref, m_ref, oh_ref, st_ref, xs_ref, *, L):
    @pl.when(pl.program_id(1) == 0)
    def _():
        c_ref[...] = c0_ref[...]
        n_ref[...] = n0_ref[...]
        m_ref[...] = m0_ref[...]
        st_ref[...] = s0_ref[...]
        xs_ref[0:HALO, :] = conv0_ref[...]

    for s in range(mq_ref.shape[0] // L):
        rows = slice(s * L, (s + 1) * L)
        _mlstm_chunk(mq_ref, mk_ref, mv_ref, gi_ref, wc_ref, bias_ref, hm_ref, c_ref, n_ref, m_ref, xs_ref,
                     L=L, rows=rows)
        _hgrn_chunk(hq_ref, hf_ref, hi_ref, la_ref, l1_ref, oml_ref, lvl_ref, oh_ref, st_ref, L=L, rows=rows)


def _recurrent(proj, gates, conv0_pad, w_conv, bias_row, c0, n0, m0, la, l1, oml, st0):
    B, T, _ = proj.shape
    L = min(RECURRENT_CHUNK, T)
    rows = min(RECURRENT_BLOCK, T)
    assert L & (L - 1) == 0 and L >= SUBLANES
    wide = lambda blk: pl.BlockSpec((None, rows, BRANCH_WIDTH), lambda b, c: (b, c, blk))
    const = lambda a: pl.BlockSpec(a.shape, lambda b, c: (0,) * a.ndim)
    state = lambda *shp: pl.BlockSpec((None,) + shp, lambda b, c: (b,) + (0,) * len(shp))
    states = [state(N_HEADS, HEAD_DIM, HEAD_DIM), state(N_HEADS, 1, HEAD_DIM), state(N_HEADS, 1, LANE)]
    hgrn_state = state(N_HEADS, HEAD_DIM, HEAD_DIM)
    lvl = jnp.asarray(_hgrn_level_index(L))
    f32 = lambda *shp: jax.ShapeDtypeStruct(shp, F32)
    return pl.pallas_call(
        functools.partial(_recurrent_kernel, L=L),
        grid=(B, T // rows),
        in_specs=[wide(C_MLQ // 4), wide(C_MLK // 4), wide(C_MLV // 4),
                  pl.BlockSpec((None, rows, 2 * LANE), lambda b, c: (b, c, 0)),
                  wide(C_HGQ // 4), wide(C_HGF // 4), wide(C_HGI // 4),
                  state(HALO, D_MODEL), const(w_conv), const(bias_row), *states,
                  const(la), const(l1), const(oml), const(lvl), hgrn_state],
        out_specs=[wide(0), *states, wide(0), hgrn_state],
        out_shape=[f32(B, T, BRANCH_WIDTH), f32(B, N_HEADS, HEAD_DIM, HEAD_DIM), f32(B, N_HEADS, 1, HEAD_DIM),
                   f32(B, N_HEADS, 1, LANE), f32(B, T, BRANCH_WIDTH), f32(B, N_HEADS, HEAD_DIM, HEAD_DIM)],
        scratch_shapes=[pltpu.VMEM((L + HALO, D_MODEL), F32)],
        compiler_params=pltpu.CompilerParams(
            dimension_semantics=("parallel", "arbitrary"), vmem_limit_bytes=VMEM_LIMIT),
        name="recurrent",
    )(proj, proj, proj, gates, proj, proj, proj, conv0_pad, w_conv, bias_row, c0, n0, m0, la, l1, oml, lvl, st0)


SB_KEY_BLOCK = 256
SB_RECENT_KEYS = 256
SB_QUERY_BLOCK = 2048
SB_ZERO_WEIGHT_LOG2 = -150.0


def _strict_upper(n):
    r = lax.broadcasted_iota(jnp.int32, (n, n), 0)
    c = lax.broadcasted_iota(jnp.int32, (n, n), 1)
    u = jnp.where(r > c, 1.0, 0.0).astype(BF16)
    return jnp.concatenate([u, u], axis=0)


def _sb_block(qb, kb, vb, carry, acc, upper, key_offset=None, valid=None):
    z = _dot(qb, kb, NT_DIMS)
    sp = jnp.maximum(z, 0.0) + _log2_1p_exp2_neg_abs(z)
    if valid is not None:
        mask = valid
        lp = jnp.where(mask, -sp, 0.0)
    elif key_offset is None:
        mask = None
        lp = -sp
    else:
        r = lax.broadcasted_iota(jnp.int32, z.shape, 0)
        c = lax.broadcasted_iota(jnp.int32, z.shape, 1)
        mask = c + key_offset < r
        lp = jnp.where(mask, -sp, 0.0)
    lp_hi = lp.astype(BF16)
    lp_lo = (lp - lp_hi.astype(F32)).astype(BF16)
    suffix = _dot(jnp.concatenate([lp_hi, lp_lo], axis=1), upper)
    a = jnp.exp2((z - sp) + suffix + carry)
    if mask is not None:
        a = jnp.where(mask, a, 0.0)
    acc = acc + _dot(a.astype(BF16), vb)
    carry = carry + jnp.sum(lp, axis=-1, keepdims=True)
    return carry, acc


def _sb_walk(qb, carry, acc, n_blocks, load_block, tk):
    upper = _strict_upper(tk)

    def cond(state):
        t, carry, _ = state
        return jnp.logical_and(t < n_blocks, jnp.max(carry) > SB_ZERO_WEIGHT_LOG2)

    def body(state):
        t, carry, acc = state
        kb, vb = load_block(n_blocks - 1 - t)
        carry, acc = _sb_block(qb, kb, vb, carry, acc, upper)
        return t + 1, carry, acc

    _, carry, acc = lax.while_loop(cond, body, (jnp.int32(0), carry, acc))
    return carry, acc


def _sb_prompt_kernel(q_ref, k_ref, v_ref, o_ref, kb_ref, vb_ref, *, tq, tk):
    i = pl.program_id(2)
    T = k_ref.shape[0]
    cast_rows = min(T, 1024)

    @pl.when(i == 0)
    def _():
        def cast(t, _):
            rows = pl.ds(pl.multiple_of(t * cast_rows, cast_rows), cast_rows)
            kb_ref[rows, :] = k_ref[rows, :].astype(BF16)
            vb_ref[rows, :] = v_ref[rows, :].astype(BF16)
            return 0
        lax.fori_loop(0, T // cast_rows, cast, 0)

    def load_block(j):
        rows = pl.ds(pl.multiple_of(j * tk, tk), tk)
        return kb_ref[rows, :], vb_ref[rows, :]

    upper = _strict_upper(tk)
    groups = []
    for s in range(tq // tk):
        blk = i * (tq // tk) + s
        qb = q_ref[s * tk:(s + 1) * tk, :]
        carry, acc = _sb_block(qb, *load_block(blk), jnp.zeros((tk, 1), F32), jnp.zeros((tk, HEAD_DIM), F32), upper,
                               key_offset=0)
        carry, acc = _sb_block(qb, *load_block(jnp.maximum(blk - 1, 0)), carry, acc, upper, valid=blk >= 1)
        groups.append((qb, carry, acc, jnp.maximum(blk - 1, 0)))
    for s, (qb, carry, acc, n_before) in enumerate(groups):
        o_ref[s * tk:(s + 1) * tk, :] = _sb_walk(qb, carry, acc, n_before, load_block, tk)[1]


def _sb_sample_kernel(q_ref, kc_ref, vc_ref, kr_ref, vr_ref, kp_hbm, vp_hbm, o_ref, kbuf_ref, vbuf_ref, sem, *, l, tk):
    tq = q_ref.shape[0]
    b, h = pl.program_id(0), pl.program_id(1)
    qb = q_ref[...]
    carry, acc = _sb_block(qb, kc_ref[...].astype(BF16), vc_ref[...].astype(BF16), jnp.zeros((tq, 1), F32),
                           jnp.zeros((tq, HEAD_DIM), F32), _strict_upper(tq), key_offset=0)

    def load_recent(j):
        rows = pl.ds(pl.multiple_of(j * tk, tk), tk)
        return kr_ref[rows, :].astype(BF16), vr_ref[rows, :].astype(BF16)

    carry, acc = _sb_walk(qb, carry, acc, kr_ref.shape[0] // tk, load_recent, tk)

    def load_older(j):
        rows = pl.ds(pl.multiple_of(j * tk, tk), tk)
        copies = [pltpu.make_async_copy(src.at[l, b, h, rows, :], dst, sem.at[slot])
                  for slot, (src, dst) in enumerate(((kp_hbm, kbuf_ref), (vp_hbm, vbuf_ref)))]
        for cp in copies:
            cp.start()
        for cp in copies:
            cp.wait()
        return kbuf_ref[...].astype(BF16), vbuf_ref[...].astype(BF16)

    n_older = (kp_hbm.shape[3] - kr_ref.shape[0]) // tk
    o_ref[...] = _sb_walk(qb, carry, acc, n_older, load_older, tk)[1]


def _sb_attention(l, q_scaled, k_rows, v_rows, k_past, v_past):
    B, T, _ = q_scaled.shape
    per_head = lambda rows: pl.BlockSpec((None, None, None, rows, HEAD_DIM), lambda b, h, i: (l, b, h, 0, 0))
    if k_past is None:
        tq = min(T, SB_QUERY_BLOCK)
        kernel_fn = functools.partial(_sb_prompt_kernel, tq=tq, tk=min(tq, SB_KEY_BLOCK))
        past_specs, past_args = [], []
        scratch = [pltpu.VMEM((T, HEAD_DIM), BF16), pltpu.VMEM((T, HEAD_DIM), BF16)]
    else:
        tq = T
        P = k_past.shape[3]
        tk, recent = min(P, SB_KEY_BLOCK), min(P, SB_RECENT_KEYS)
        kernel_fn = functools.partial(_sb_sample_kernel, l=l, tk=tk)
        newest = pl.BlockSpec((None, None, None, recent, HEAD_DIM), lambda b, h, i: (l, b, h, P // recent - 1, 0))
        in_hbm = pl.BlockSpec(memory_space=pl.ANY)
        past_specs, past_args = [newest, newest, in_hbm, in_hbm], [k_past, v_past, k_past, v_past]
        scratch = [pltpu.VMEM((tk, HEAD_DIM), F32), pltpu.VMEM((tk, HEAD_DIM), F32), pltpu.SemaphoreType.DMA((2,))]
    return pl.pallas_call(
        kernel_fn,
        grid=(B, N_HEADS, T // tq),
        in_specs=[pl.BlockSpec((None, tq, HEAD_DIM), lambda b, h, i: (b, i, h)), per_head(T), per_head(T)]
        + past_specs,
        out_specs=pl.BlockSpec((None, tq, HEAD_DIM), lambda b, h, i: (b, i, h)),
        out_shape=jax.ShapeDtypeStruct((B, T, BRANCH_WIDTH), F32),
        scratch_shapes=scratch,
        compiler_params=pltpu.CompilerParams(
            dimension_semantics=("parallel", "parallel", "arbitrary"), vmem_limit_bytes=VMEM_LIMIT),
        name="stickbreak",
    )(q_scaled, k_rows, v_rows, *past_args)


def _merge_kernel(hm_ref, os_ref, oh_ref, x_ref, p_ref, gpre_ref, wg0_ref, wg1_ref, wg2_ref, gmlh_ref, ghgh_ref,
                  wb_ref, wo_ref, gpost_ref, wpg_ref, wpp_ref, r_ref):
    wg_refs = (wg0_ref, wg1_ref, wg2_ref)

    def head_norm(a):
        parts = []
        for h in range(N_HEADS):
            seg = a[:, h * HEAD_DIM:(h + 1) * HEAD_DIM]
            parts.append(seg * lax.rsqrt(jnp.mean(seg * seg, axis=-1, keepdims=True) + RMS_EPS))
        return jnp.concatenate(parts, axis=-1)

    x = x_ref[...]
    u = _rms_norm_bf16(x, gpre_ref[...])
    def proj(where, width, skip=0):
        run, start = where
        return _dot(u, wg_refs[run][:, start + skip:start + skip + width])

    silu = lambda a: a * _sigmoid(a)
    h_ml = head_norm(hm_ref[...]) * gmlh_ref[...] * _sigmoid(proj(G_MLO, W)) * silu(proj(G_MLZ, W))
    h_sb = os_ref[...] * silu(proj(G_SBZ, W))
    h_hg = head_norm(oh_ref[...]) * ghgh_ref[...] * silu(proj(G_HGZ, W))
    branches = [h.astype(BF16) for h in (h_ml, h_sb, h_hg)]
    half = D_MODEL // 2
    gated = lambda b, c0: (_sigmoid(proj(G_GATES, half, b * D_MODEL + c0))
                           * _dot(branches[b], wb_ref[b, :, c0:c0 + half]))
    merged = jnp.concatenate(
        [(gated(0, c0) + gated(1, c0) + gated(2, c0)).astype(BF16) for c0 in (0, half)], axis=-1)
    y = _dot(merged, wo_ref[...])
    r = x + y * lax.rsqrt(jnp.mean(y * y, axis=-1, keepdims=True) + RMS_EPS) * gpost_ref[...]
    gate = _sigmoid(_dot(r.astype(BF16), wpg_ref[...]))
    r_ref[...] = r + gate * _dot(p_ref[...].astype(BF16), wpp_ref[...])


def _merge(l, hm, o_s, o_h, x2d, p_all, lw):
    n = x2d.shape[0]
    tm = min(n, MERGE_ROWS)
    rows = lambda width: pl.BlockSpec((tm, width), lambda i: (i, 0))
    full = lambda a: pl.BlockSpec(a.shape, lambda i: (0,) * a.ndim, pipeline_mode=pl.Buffered(1))
    params = [lw["g_pre"], *lw["w_gate"]] + [lw[k] for k in ("g_mlh", "g_hgh", "w_branch", "w_out", "g_post", "w_pg",
                                                              "w_pp")]
    return pl.pallas_call(
        _merge_kernel,
        grid=(n // tm,),
        in_specs=[rows(W), rows(W), rows(W), rows(D_MODEL), pl.BlockSpec((None, tm, PLE_DIM), lambda i: (l, i, 0))]
        + [full(a) for a in params],
        out_specs=rows(D_MODEL),
        out_shape=jax.ShapeDtypeStruct((n, D_MODEL), F32),
        compiler_params=pltpu.CompilerParams(
            dimension_semantics=("parallel",), vmem_limit_bytes=VMEM_LIMIT),
        name="merge",
    )(hm, o_s, o_h, x2d, p_all, *params)


def _layer(l, depth, x, p_all, c0, n0, m0, conv0, s0, k_past, v_past, kv_prev, lw):
    B, T, _ = x.shape
    proj2d, gates2d, sbq2d, k_rows, v_rows = _inproj(l, depth, x, lw["g_pre"], lw["w_in"], lw["w_if"], kv_prev)
    proj = proj2d.reshape(B, T, N_PROJ)

    conv0_pad = jnp.pad(conv0, ((0, 0), (HALO - (CONV_WIDTH - 1), 0), (0, 0)))
    hm, c_new, n_new, m_new, o_h, st_new = _recurrent(
        proj, gates2d.reshape(B, T, 2 * LANE), conv0_pad, lw["w_conv"], lw["bias_row"], c0,
        n0.reshape(B, N_HEADS, 1, HEAD_DIM),
        jnp.broadcast_to(m0[:, :, None, None], (B, N_HEADS, 1, LANE)),
        lw["log_lb"], lw["log1m_lb"], lw["one_m_lb"], jnp.swapaxes(s0, -1, -2))
    o_s = _sb_attention(l, sbq2d.reshape(B, T, BRANCH_WIDTH), k_rows, v_rows, k_past, v_past)

    flat = lambda a: a.reshape(B * T, a.shape[-1])
    r = _merge(l, flat(hm), flat(o_s), flat(o_h), flat(x), p_all.reshape(depth, B * T, PLE_DIM), lw)

    conv_new = proj[:, T - (CONV_WIDTH - 1):, C_MLQ * LANE:C_MLQ * LANE + 2 * BRANCH_WIDTH]
    states = (c_new, n_new[:, :, 0, :], m_new[:, :, 0, 0], conv_new, jnp.swapaxes(st_new, -1, -2))
    return r.reshape(B, T, D_MODEL), states, (k_rows, v_rows)


def _layer_weights(l, lower_bounds, g_pre, w_in, b_i, b_f, w_conv, g_mlh, g_hgh, w_branch, w_out, g_post, w_pg, w_pp):
    cols = lambda span: w_in[l, :, span[0]:span[1]].astype(BF16)
    lane_pad = lambda a: jnp.pad(a, ((0, 0), (0, LANE - a.shape[1])))
    w_if = jnp.concatenate([lane_pad(cols(SRC_ML_I)), lane_pad(cols(SRC_ML_F))], axis=1)
    w_re = jnp.stack([cols(span) for span in INPROJ_SOURCES])
    w_gate = [cols(span) for span in MERGE_SOURCES]
    bias_row = jnp.concatenate([jnp.pad(b_i[l], (0, LANE - N_HEADS)), jnp.pad(b_f[l], (0, LANE - N_HEADS))])
    bias_row = bias_row.reshape(1, 2 * LANE)
    lb = lower_bounds[l].reshape(1, BRANCH_WIDTH)
    return dict(
        g_pre=g_pre[l].reshape(1, D_MODEL), w_in=w_re, w_if=w_if, w_gate=w_gate, bias_row=bias_row,
        w_conv=w_conv[l],
        g_mlh=g_mlh[l].reshape(1, BRANCH_WIDTH), g_hgh=g_hgh[l].reshape(1, BRANCH_WIDTH),
        log_lb=jnp.log(lb) * LOG2_E, log1m_lb=jnp.log1p(-lb) * LOG2_E, one_m_lb=1.0 - lb,
        w_branch=w_branch[l].astype(BF16), w_out=w_out[l].astype(BF16), g_post=g_post[l].reshape(1, D_MODEL),
        w_pg=w_pg[l].astype(BF16), w_pp=w_pp[l].astype(BF16))


def kernel(x_prompt, x_sample, p_prompt, p_sample, state_mlstm_C, state_mlstm_n, state_mlstm_m, state_mlstm_conv,
           cache_sb_k, cache_sb_v, state_hgrn_S, g_pre, w_in, b_mlstm_i, b_mlstm_f, w_mlstm_conv, g_mlstm_head,
           hgrn_lb_logits, g_hgrn_head, w_branch, w_out, g_post, w_ple_gate, w_ple_proj):
    depth = w_in.shape[0]
    lb_cum = jnp.cumsum(jax.nn.softmax(hgrn_lb_logits.astype(F32), axis=0), axis=0)
    lower_bounds = lb_cum - lb_cum[:1]
    B = x_prompt.shape[0]
    yp, ys = x_prompt, x_sample
    new_p, new_s = [], []
    kv_p = kv_s = None
    for l in range(depth):
        lw = _layer_weights(l, lower_bounds, g_pre, w_in, b_mlstm_i, b_mlstm_f, w_mlstm_conv, g_mlstm_head,
                            g_hgrn_head, w_branch, w_out, g_post, w_ple_gate, w_ple_proj)
        zeros = lambda *shape: jnp.zeros(shape, F32)
        yp, st, kv_p = _layer(l, depth, yp, p_prompt, zeros(B, N_HEADS, HEAD_DIM, HEAD_DIM),
                              zeros(B, N_HEADS, HEAD_DIM), zeros(B, N_HEADS),
                              zeros(B, CONV_WIDTH - 1, 2 * BRANCH_WIDTH), zeros(B, N_HEADS, HEAD_DIM, HEAD_DIM),
                              None, None, kv_p, lw)
        new_p.append(st)
        ys, st, kv_s = _layer(l, depth, ys, p_sample, state_mlstm_C[l], state_mlstm_n[l], state_mlstm_m[l],
                              state_mlstm_conv[l], state_hgrn_S[l], cache_sb_k, cache_sb_v, kv_s, lw)
        new_s.append(st)
    stack = lambda sts, i: jnp.stack([s[i] for s in sts])
    outs = lambda sts, kv: (stack(sts, 0), stack(sts, 1), stack(sts, 2), stack(sts, 3), kv[0], kv[1], stack(sts, 4))
    return (yp, ys, *outs(new_p, kv_p), *outs(new_s, kv_s))
```

```python
import functools

import jax
import jax.numpy as jnp
import numpy as np
from jax import lax
from jax.experimental import pallas as pl
from jax.experimental.pallas import tpu as pltpu

F32 = jnp.float32
BF16 = jnp.bfloat16

D_MODEL = 1024
PLE_DIM = 256
N_HEADS = 4
BRANCH_WIDTH = D_MODEL // 2
HEAD_DIM = BRANCH_WIDTH // N_HEADS
CONV_WIDTH = 4
RMS_EPS = 1e-6
LANE = 128
SUBLANES = 8
HALO = SUBLANES
NEG_BIG = -1e30
LOG2_E = 1.4426950408889634
RECURRENT_CHUNK = 128
RECURRENT_BLOCK = 1024
MERGE_ROWS = 512

W = BRANCH_WIDTH
SRC_ML_QKV = (0, 3 * W)
SRC_ML_OZ = (3 * W, 5 * W)
SRC_ML_I = (5 * W, 5 * W + N_HEADS)
SRC_ML_F = (5 * W + N_HEADS, 5 * W + 2 * N_HEADS)
_S = 5 * W + 2 * N_HEADS
SRC_SB_QKV = (_S, _S + 3 * W)
SRC_SB_Z = (_S + 3 * W, _S + 4 * W)
SRC_HG_QFI = (_S + 4 * W, _S + 7 * W)
SRC_HG_Z_GATES = (_S + 7 * W, _S + 8 * W + 3 * D_MODEL)

INPROJ_TILE = 3 * BRANCH_WIDTH
INPROJ_SOURCES = (SRC_ML_QKV, SRC_HG_QFI, SRC_SB_QKV)
SB_TILE = 2
N_PROJ = SB_TILE * INPROJ_TILE
C_MLQ, C_MLK, C_MLV, C_HGQ, C_HGF, C_HGI = 0, 4, 8, 12, 16, 20
MERGE_SOURCES = (SRC_ML_OZ, SRC_SB_Z, SRC_HG_Z_GATES)
G_MLO, G_MLZ, G_SBZ, G_HGZ, G_GATES = (0, 0), (0, W), (1, 0), (2, 0), (2, W)

VMEM_LIMIT = 56 * 1024 * 1024

NT_DIMS = (((1,), (1,)), ((), ()))
TN_DIMS = (((0,), (0,)), ((), ()))


def _sigmoid(x):
    return 1.0 / (1.0 + jnp.exp(-x))


def _softplus_neg_abs(x):
    return jnp.log(1.0 + jnp.exp(-jnp.abs(x)))


def _log2_1p_exp2_neg_abs(x):
    return jnp.log2(1.0 + jnp.exp2(-jnp.abs(x)))


def _log_sigmoid(x):
    return jnp.minimum(x, 0.0) - _softplus_neg_abs(x)


def _dot(a, b, dims=None):
    if dims is None:
        dims = (((a.ndim - 1,), (0,)), ((), ()))
    return lax.dot_general(a, b, dims, preferred_element_type=F32)


def _split3(x):
    hi = x.astype(BF16).astype(F32)
    r = x - hi
    mid = r.astype(BF16).astype(F32)
    return hi, mid, r - mid


def _rms_norm_bf16(x, g):
    ms = jnp.mean(x * x, axis=-1, keepdims=True)
    return (x * lax.rsqrt(ms + RMS_EPS) * g).astype(BF16)


def _inproj_kernel(*refs, aliased):
    x_ref, g_ref, w_ref, wif_ref = refs[:4]
    o_ref, gi_ref, sbq_ref, k_ref, v_ref, u_ref = refs[-6:]
    j = pl.program_id(1)

    @pl.when(j == 0)
    def _():
        u_ref[...] = _rms_norm_bf16(x_ref[...], g_ref[...])
        gi_ref[...] = jnp.dot(u_ref[...], wif_ref[...], preferred_element_type=F32)

    res = jnp.dot(u_ref[...], w_ref[...], preferred_element_type=F32)

    @pl.when(j < SB_TILE)
    def _():
        o_ref[...] = res

    @pl.when(j == SB_TILE)
    def _():
        sbq_ref[...] = (res[:, :W] * (HEAD_DIM ** -0.5 * LOG2_E)).astype(BF16)
        if aliased:
            k_dst, v_dst = k_ref, v_ref
        else:
            k_dst, v_dst = k_ref.at[0], v_ref.at[0]
            if k_ref.shape[0] > 1:
                k_ref[1:] = jnp.zeros((k_ref.shape[0] - 1,) + k_ref.shape[1:], F32)
                v_ref[1:] = jnp.zeros((v_ref.shape[0] - 1,) + v_ref.shape[1:], F32)
        bt, _, tt, _ = k_dst.shape
        for h in range(N_HEADS):
            k_dst[:, h] = res[:, W + h * HEAD_DIM:W + (h + 1) * HEAD_DIM].reshape(bt, tt, HEAD_DIM)
            v_dst[:, h] = res[:, 2 * W + h * HEAD_DIM:2 * W + (h + 1) * HEAD_DIM].reshape(bt, tt, HEAD_DIM)


def _inproj(l, depth, x, g_pre, w_tiles, w_if, kv_prev):
    B, T, _ = x.shape
    n = B * T
    tm = min(n, 1024)
    bt, tt = max(1, tm // T), min(tm, T)
    aliased = kv_prev is not None
    assert aliased == (l > 0)
    kv_index = lambda i, j: ((i * tm) // T // bt, 0, ((i * tm) % T) // tt, 0)
    if aliased:
        kv_block = pl.BlockSpec((None, bt, N_HEADS, tt, HEAD_DIM), lambda i, j: (l,) + kv_index(i, j))
    else:
        kv_block = pl.BlockSpec((depth, bt, N_HEADS, tt, HEAD_DIM), lambda i, j: (0,) + kv_index(i, j))
    kv_shape = jax.ShapeDtypeStruct((depth, B, N_HEADS, T, HEAD_DIM), F32)
    prev_specs = [pl.BlockSpec(memory_space=pl.ANY)] * 2 if aliased else []
    return pl.pallas_call(
        functools.partial(_inproj_kernel, aliased=aliased),
        grid=(n // tm, len(INPROJ_SOURCES)),
        in_specs=[
            pl.BlockSpec((tm, D_MODEL), lambda i, j: (i, 0)),
            pl.BlockSpec((1, D_MODEL), lambda i, j: (0, 0)),
            pl.BlockSpec((None, D_MODEL, INPROJ_TILE), lambda i, j: (j, 0, 0)),
            pl.BlockSpec((D_MODEL, 2 * LANE), lambda i, j: (0, 0)),
        ] + prev_specs,
        out_specs=[
            pl.BlockSpec((tm, INPROJ_TILE), lambda i, j: (i, jnp.minimum(j, SB_TILE - 1))),
            pl.BlockSpec((tm, 2 * LANE), lambda i, j: (i, 0)),
            pl.BlockSpec((tm, W), lambda i, j: (i, 0)),
            kv_block, kv_block,
        ],
        out_shape=[jax.ShapeDtypeStruct((n, N_PROJ), F32), jax.ShapeDtypeStruct((n, 2 * LANE), F32),
                   jax.ShapeDtypeStruct((n, W), BF16), kv_shape, kv_shape],
        input_output_aliases={4: 3, 5: 4} if aliased else {},
        scratch_shapes=[pltpu.VMEM((tm, D_MODEL), BF16)],
        compiler_params=pltpu.CompilerParams(
            dimension_semantics=("parallel", "arbitrary"), vmem_limit_bytes=VMEM_LIMIT),
        name="inproj",
    )(x.reshape(n, D_MODEL), g_pre.reshape(1, D_MODEL), w_tiles, w_if, *(kv_prev if aliased else ()))


def _mlstm_chunk(q_ref, k_ref, v_ref, gi_ref, wc_ref, bias_ref, hm_ref, c_ref, n_ref, m_ref, xs_ref, *, L, rows):
    xs_ref[HALO:HALO + L, 0:BRANCH_WIDTH] = q_ref[rows, :]
    xs_ref[HALO:HALO + L, BRANCH_WIDTH:] = k_ref[rows, :]
    base = HALO - (CONV_WIDTH - 1)
    qk = xs_ref[base:base + L, :] * wc_ref[0:1, :]
    for j in range(1, CONV_WIDTH):
        qk = qk + xs_ref[base + j:base + j + L, :] * wc_ref[j:j + 1, :]
    qk = qk * _sigmoid(qk)
    xs_ref[0:HALO, :] = xs_ref[L:L + HALO, :]

    gates = gi_ref[rows, :] + bias_ref[...]
    ig_all = gates[:, :LANE]
    row = lax.broadcasted_iota(jnp.int32, (L, L), 0)
    col = lax.broadcasted_iota(jnp.int32, (L, L), 1)
    causal = col <= row
    ltri = jnp.where(causal, 1.0, 0.0).astype(BF16)
    bcum = sum(_dot(ltri, piece.astype(BF16)) for piece in _split3(_log_sigmoid(gates[:, LANE:])))
    key_term = jnp.transpose(ig_all - bcum)
    scale = HEAD_DIM ** -0.5

    for h in range(N_HEADS):
        ig = ig_all[:, h:h + 1]
        b = bcum[:, h:h + 1]
        dmat = jnp.where(causal, b + key_term[h:h + 1, :], NEG_BIG)
        m_prev = m_ref[h][:, 0:1]
        inter = b + m_prev
        m_t = jnp.maximum(jnp.max(dmat, axis=-1, keepdims=True), inter)
        qh = qk[:, h * HEAD_DIM:(h + 1) * HEAD_DIM]
        kh = qk[:, BRANCH_WIDTH + h * HEAD_DIM:BRANCH_WIDTH + (h + 1) * HEAD_DIM] * scale
        vh = v_ref[rows, h * HEAD_DIM:(h + 1) * HEAD_DIM]
        qb, kb, vb = qh.astype(BF16), kh.astype(BF16), vh.astype(BF16)
        s = _dot(qb, kb, NT_DIMS) * jnp.exp(dmat - m_t)
        w_inter = jnp.exp(inter - m_t)
        c_prev = c_ref[h]
        n_prev = n_ref[h]
        num = _dot(s.astype(BF16), vb) + w_inter * _dot(qb, c_prev.astype(BF16))
        den = jnp.sum(s, axis=-1, keepdims=True) + w_inter * jnp.sum(qh * n_prev, axis=-1, keepdims=True)
        hm_ref[rows, h * HEAD_DIM:(h + 1) * HEAD_DIM] = num /jnp.maximum(jnp.abs(den), jnp.exp(-m_t))
        m_new = m_t[L - 1:L, :]
        b_last = b[L - 1:L, :]
        dec_s = jnp.exp(b_last - b + ig - m_new)
        dec_c = jnp.exp(b_last + m_prev - m_new)
        kd = kh * dec_s
        c_ref[h] = dec_c * c_prev + _dot(kd.astype(BF16), vb, TN_DIMS)
        n_ref[h] = dec_c * n_prev + jnp.sum(kd, axis=0, keepdims=True)
        m_ref[h] = jnp.broadcast_to(m_new, (1, LANE))


def _hgrn_level_operand(lv, bc, g, kk, q_all, L):
    width = bc.shape[-1]
    rowi = lax.broadcasted_iota(jnp.int32, (L, width), 0)
    first_half = (rowi // lv) % 2 == 0
    if lv == 1:
        e = jnp.where(first_half, 0.0, g)
    elif 2 * lv >= SUBLANES:
        b3 = bc.reshape(L // (2 * lv), 2 * lv, width)
        e = (b3 - b3[:, lv - 1:lv, :]).reshape(L, width)
    else:
        b3 = bc.reshape(L // SUBLANES, SUBLANES, width)
        sub = lax.broadcasted_iota(jnp.int32, b3.shape, 1)
        rho = b3[:, lv - 1:lv, :]
        for start in range(2 * lv, SUBLANES, 2 * lv):
            rho = jnp.where(sub >= start, b3[:, start + lv - 1:start + lv, :], rho)
        e = (b3 - rho).reshape(L, width)
    return (jnp.where(first_half, kk, q_all) * jnp.exp2(-jnp.abs(e))).astype(BF16)


def _hgrn_chunk(q_ref, f_ref, i_ref, la_ref, l1_ref, oml_ref, lvl_ref, o_ref, st_ref, *, L, rows):
    f = f_ref[rows, :]
    la = la_ref[...]
    f = f * LOG2_E
    lb_term = l1_ref[...] + (jnp.minimum(f, 0.0) - _log2_1p_exp2_neg_abs(f))
    g = jnp.maximum(la, lb_term) + _log2_1p_exp2_neg_abs(la - lb_term)
    kk = oml_ref[...] / (1.0 + jnp.exp2(f))
    q_all = q_ref[rows, :]

    row = lax.broadcasted_iota(jnp.int32, (L, L), 0)
    col = lax.broadcasted_iota(jnp.int32, (L, L), 1)
    ltri = jnp.where(col <= row, 1.0, 0.0).astype(BF16)
    bc = sum(_dot(ltri, piece.astype(BF16)) for piece in _split3(g))
    bc_last = bc[L - 1:L, :]
    qe = (q_all * jnp.exp2(bc)).astype(BF16)
    kdec = (kk * jnp.exp2(bc_last - bc)).astype(BF16)
    dec_state = jnp.exp2(bc_last)
    n_levels = L.bit_length() - 1
    operands = [_hgrn_level_operand(1 << p, bc, g, kk, q_all, L) for p in range(n_levels)]
    lvl = lvl_ref[...]

    for h in range(N_HEADS):
        sl = slice(h * HEAD_DIM, (h + 1) * HEAD_DIM)
        ih = i_ref[rows, sl]
        ib = ih.astype(BF16)
        st = st_ref[h]
        a = jnp.zeros((L, L), F32)
        for p, y in enumerate(operands):
            a = jnp.where(lvl == p, _dot(y[:, sl], y[:, sl], NT_DIMS), a)
        diag = jnp.sum(q_all[:, sl] * kk[:, sl], axis=-1, keepdims=True)
        o_ref[rows, sl] = _dot(a.astype(BF16), ib) + _dot(qe[:, sl], st.astype(BF16), NT_DIMS) + diag * ih
        st_ref[h] = st * dec_state[:, sl] + _dot(ib, kdec[:, sl], TN_DIMS)


def _hgrn_level_index(L):
    t = np.arange(L)[:, None]
    s = np.arange(L)[None, :]
    x = np.maximum(t ^ s, 1)
    return np.where(s < t, np.floor(np.log2(x)), -1).astype(np.int32)


def _recurrent_kernel(mq_ref, mk_ref, mv_ref, gi_ref, hq_ref, hf_ref, hi_ref, conv0_ref, wc_ref, bias_ref,
                      c0_ref, n0_ref, m0_ref, la_ref, l1_ref, oml_ref, lvl_ref, s0_ref,
                      hm_ref, c_ref, n_ref, m_ref, oh_ref, st_ref, xs_ref, *, L):
    @pl.when(pl.program_id(1) == 0)
    def _():
        c_ref[...] = c0_ref[...]
        n_ref[...] = n0_ref[...]
        m_ref[...] = m0_ref[...]
        st_ref[...] = s0_ref[...]
        xs_ref[0:HALO, :] = conv0_ref[...]

    for s in range(mq_ref.shape[0] // L):
        rows = slice(s * L, (s + 1) * L)
        _mlstm_chunk(mq_ref, mk_ref, mv_ref, gi_ref, wc_ref, bias_ref, hm_ref, c_ref, n_ref, m_ref, xs_ref,
                     L=L, rows=rows)
        _hgrn_chunk(hq_ref, hf_ref, hi_ref, la_ref, l1_ref, oml_ref, lvl_ref, oh_ref, st_ref, L=L, rows=rows)


def _recurrent(proj, gates, conv0_pad, w_conv, bias_row, c0, n0, m0, la, l1, oml, st0):
    B, T, _ = proj.shape
    L = min(RECURRENT_CHUNK, T)
    rows = min(RECURRENT_BLOCK, T)
    assert L & (L - 1) == 0 and L >= SUBLANES
    wide = lambda blk: pl.BlockSpec((None, rows, BRANCH_WIDTH), lambda b, c: (b, c, blk))
    const = lambda a: pl.BlockSpec(a.shape, lambda b, c: (0,) * a.ndim)
    state = lambda *shp: pl.BlockSpec((None,) + shp, lambda b, c: (b,) + (0,) * len(shp))
    states = [state(N_HEADS, HEAD_DIM, HEAD_DIM), state(N_HEADS, 1, HEAD_DIM), state(N_HEADS, 1, LANE)]
    hgrn_state = state(N_HEADS, HEAD_DIM, HEAD_DIM)
    lvl = jnp.asarray(_hgrn_level_index(L))
    f32 = lambda *shp: jax.ShapeDtypeStruct(shp, F32)
    return pl.pallas_call(
        functools.partial(_recurrent_kernel, L=L),
        grid=(B, T // rows),
        in_specs=[wide(C_MLQ // 4), wide(C_MLK // 4), wide(C_MLV // 4),
                  pl.BlockSpec((None, rows, 2 * LANE), lambda b, c: (b, c, 0)),
                  wide(C_HGQ // 4), wide(C_HGF // 4), wide(C_HGI // 4),
                  state(HALO, D_MODEL), const(w_conv), const(bias_row), *states,
                  const(la), const(l1), const(oml), const(lvl), hgrn_state],
        out_specs=[wide(0), *states, wide(0), hgrn_state],
        out_shape=[f32(B, T, BRANCH_WIDTH), f32(B, N_HEADS, HEAD_DIM, HEAD_DIM), f32(B, N_HEADS, 1, HEAD_DIM),
                   f32(B, N_HEADS, 1, LANE), f32(B, T, BRANCH_WIDTH), f32(B, N_HEADS, HEAD_DIM, HEAD_DIM)],
        scratch_shapes=[pltpu.VMEM((L + HALO, D_MODEL), F32)],
        compiler_params=pltpu.CompilerParams(
            dimension_semantics=("parallel", "arbitrary"), vmem_limit_bytes=VMEM_LIMIT),
        name="recurrent",
    )(proj, proj, proj, gates, proj, proj, proj, conv0_pad, w_conv, bias_row, c0, n0, m0, la, l1, oml, lvl, st0)


SB_KEY_BLOCK = 256
SB_RECENT_KEYS = 256
SB_QUERY_BLOCK = 2048
SB_ZERO_WEIGHT_LOG2 = -150.0


def _strict_upper(n):
    r = lax.broadcasted_iota(jnp.int32, (n, n), 0)
    c = lax.broadcasted_iota(jnp.int32, (n, n), 1)
    u = jnp.where(r > c, 1.0, 0.0).astype(BF16)
    return jnp.concatenate([u, u], axis=0)


def _sb_block(qb, kb, vb, carry, acc, upper, key_offset=None, valid=None):
    z = _dot(qb, kb, NT_DIMS)
    sp = jnp.maximum(z, 0.0) + _log2_1p_exp2_neg_abs(z)
    if valid is not None:
        mask = valid
        lp = jnp.where(mask, -sp, 0.0)
    elif key_offset is None:
        mask = None
        lp = -sp
    else:
        r = lax.broadcasted_iota(jnp.int32, z.shape, 0)
        c = lax.broadcasted_iota(jnp.int32, z.shape, 1)
        mask = c + key_offset < r
        lp = jnp.where(mask, -sp, 0.0)
    lp_hi = lp.astype(BF16)
    lp_lo = (lp - lp_hi.astype(F32)).astype(BF16)
    suffix = _dot(jnp.concatenate([lp_hi, lp_lo], axis=1), upper)
    a = jnp.exp2((z - sp) + suffix + carry)
    if mask is not None:
        a = jnp.where(mask, a, 0.0)
    acc = acc + _dot(a.astype(BF16), vb)
    carry = carry + jnp.sum(lp, axis=-1, keepdims=True)
    return carry, acc


def _sb_walk(qb, carry, acc, n_blocks, load_block, tk):
    upper = _strict_upper(tk)

    def cond(state):
        t, carry, _ = state
        return jnp.logical_and(t < n_blocks, jnp.max(carry) > SB_ZERO_WEIGHT_LOG2)

    def body(state):
        t, carry, acc = state
        kb, vb = load_block(n_blocks - 1 - t)
        carry, acc = _sb_block(qb, kb, vb, carry, acc, upper)
        return t + 1, carry, acc

    _, carry, acc = lax.while_loop(cond, body, (jnp.int32(0), carry, acc))
    return carry, acc


def _sb_prompt_kernel(q_ref, k_ref, v_ref, o_ref, kb_ref, vb_ref, *, tq, tk):
    i = pl.program_id(2)
    T = k_ref.shape[0]
    cast_rows = min(T, 1024)

    @pl.when(i == 0)
    def _():
        def cast(t, _):
            rows = pl.ds(pl.multiple_of(t * cast_rows, cast_rows), cast_rows)
            kb_ref[rows, :] = k_ref[rows, :].astype(BF16)
            vb_ref[rows, :] = v_ref[rows, :].astype(BF16)
            return 0
        lax.fori_loop(0, T // cast_rows, cast, 0)

    def load_block(j):
        rows = pl.ds(pl.multiple_of(j * tk, tk), tk)
        return kb_ref[rows, :], vb_ref[rows, :]

    upper = _strict_upper(tk)
    groups = []
    for s in range(tq // tk):
        blk = i * (tq // tk) + s
        qb = q_ref[s * tk:(s + 1) * tk, :]
        hk = tk // 2
        kb_own, vb_own = load_block(blk)
        zero = lambda n: (jnp.zeros((n, 1), F32), jnp.zeros((n, HEAD_DIM), F32))
        upper_h = _strict_upper(hk)
        c_top, a_top = _sb_block(qb[:hk], kb_own[:hk], vb_own[:hk], *zero(hk), upper_h, key_offset=0)
        c_bot, a_bot = _sb_block(qb[hk:], kb_own[hk:], vb_own[hk:], *zero(hk), upper_h, key_offset=0)
        c_bot, a_bot = _sb_block(qb[hk:], kb_own[:hk], vb_own[:hk], c_bot, a_bot, upper_h)
        carry, acc = jnp.concatenate([c_top, c_bot], axis=0), jnp.concatenate([a_top, a_bot], axis=0)
        carry, acc = _sb_block(qb, *load_block(jnp.maximum(blk - 1, 0)), carry, acc, upper, valid=blk >= 1)
        groups.append((qb, carry, acc, jnp.maximum(blk - 1, 0)))
    for s, (qb, carry, acc, n_before) in enumerate(groups):
        o_ref[s * tk:(s + 1) * tk, :] = _sb_walk(qb, carry, acc, n_before, load_block, tk)[1]


def _sb_sample_kernel(q_ref, kc_ref, vc_ref, kr_ref, vr_ref, kp_hbm, vp_hbm, o_ref, kbuf_ref, vbuf_ref, sem, *, l, tk):
    tq = q_ref.shape[0]
    b, h = pl.program_id(0), pl.program_id(1)
    qb = q_ref[...]
    carry, acc = _sb_block(qb, kc_ref[...].astype(BF16), vc_ref[...].astype(BF16), jnp.zeros((tq, 1), F32),
                           jnp.zeros((tq, HEAD_DIM), F32), _strict_upper(tq), key_offset=0)

    def load_recent(j):
        rows = pl.ds(pl.multiple_of(j * tk, tk), tk)
        return kr_ref[rows, :].astype(BF16), vr_ref[rows, :].astype(BF16)

    carry, acc = _sb_walk(qb, carry, acc, kr_ref.shape[0] // tk, load_recent, tk)

    def load_older(j):
        rows = pl.ds(pl.multiple_of(j * tk, tk), tk)
        copies = [pltpu.make_async_copy(src.at[l, b, h, rows, :], dst, sem.at[slot])
                  for slot, (src, dst) in enumerate(((kp_hbm, kbuf_ref), (vp_hbm, vbuf_ref)))]
        for cp in copies:
            cp.start()
        for cp in copies:
            cp.wait()
        return kbuf_ref[...].astype(BF16), vbuf_ref[...].astype(BF16)

    n_older = (kp_hbm.shape[3] - kr_ref.shape[0]) // tk
    o_ref[...] = _sb_walk(qb, carry, acc, n_older, load_older, tk)[1]


def _sb_attention(l, q_scaled, k_rows, v_rows, k_past, v_past):
    B, T, _ = q_scaled.shape
    per_head = lambda rows: pl.BlockSpec((None, None, None, rows, HEAD_DIM), lambda b, h, i: (l, b, h, 0, 0))
    if k_past is None:
        tq = min(T, SB_QUERY_BLOCK)
        kernel_fn = functools.partial(_sb_prompt_kernel, tq=tq, tk=min(tq, SB_KEY_BLOCK))
        past_specs, past_args = [], []
        scratch = [pltpu.VMEM((T, HEAD_DIM), BF16), pltpu.VMEM((T, HEAD_DIM), BF16)]
    else:
        tq = T
        P = k_past.shape[3]
        tk, recent = min(P, SB_KEY_BLOCK), min(P, SB_RECENT_KEYS)
        kernel_fn = functools.partial(_sb_sample_kernel, l=l, tk=tk)
        newest = pl.BlockSpec((None, None, None, recent, HEAD_DIM), lambda b, h, i: (l, b, h, P // recent - 1, 0))
        in_hbm = pl.BlockSpec(memory_space=pl.ANY)
        past_specs, past_args = [newest, newest, in_hbm, in_hbm], [k_past, v_past, k_past, v_past]
        scratch = [pltpu.VMEM((tk, HEAD_DIM), F32), pltpu.VMEM((tk, HEAD_DIM), F32), pltpu.SemaphoreType.DMA((2,))]
    return pl.pallas_call(
        kernel_fn,
        grid=(B, N_HEADS, T // tq),
        in_specs=[pl.BlockSpec((None, tq, HEAD_DIM), lambda b, h, i: (b, i, h)), per_head(T), per_head(T)]
        + past_specs,
        out_specs=pl.BlockSpec((None, tq, HEAD_DIM), lambda b, h, i: (b, i, h)),
        out_shape=jax.ShapeDtypeStruct((B, T, BRANCH_WIDTH), F32),
        scratch_shapes=scratch,
        compiler_params=pltpu.CompilerParams(
            dimension_semantics=("parallel", "parallel", "arbitrary"), vmem_limit_bytes=VMEM_LIMIT),
        name="stickbreak",
    )(q_scaled, k_rows, v_rows, *past_args)


def _merge_kernel(hm_ref, os_ref, oh_ref, x_ref, p_ref, gpre_ref, wg0_ref, wg1_ref, wg2_ref, gmlh_ref, ghgh_ref,
                  wb_ref, wo_ref, gpost_ref, wpg_ref, wpp_ref, r_ref):
    wg_refs = (wg0_ref, wg1_ref, wg2_ref)

    def head_norm(a):
        parts = []
        for h in range(N_HEADS):
            seg = a[:, h * HEAD_DIM:(h + 1) * HEAD_DIM]
            parts.append(seg * lax.rsqrt(jnp.mean(seg * seg, axis=-1, keepdims=True) + RMS_EPS))
        return jnp.concatenate(parts, axis=-1)

    x = x_ref[...]
    u = _rms_norm_bf16(x, gpre_ref[...])
    def proj(where, width, skip=0):
        run, start = where
        return _dot(u, wg_refs[run][:, start + skip:start + skip + width])

    silu = lambda a: a * _sigmoid(a)
    h_ml = head_norm(hm_ref[...]) * gmlh_ref[...] * _sigmoid(proj(G_MLO, W)) * silu(proj(G_MLZ, W))
    h_sb = os_ref[...] * silu(proj(G_SBZ, W))
    h_hg = head_norm(oh_ref[...]) * ghgh_ref[...] * silu(proj(G_HGZ, W))
    merged = (_sigmoid(proj(G_GATES, D_MODEL)) * _dot(h_ml.astype(BF16), wb_ref[0])
              + _sigmoid(proj(G_GATES, D_MODEL, D_MODEL)) * _dot(h_sb.astype(BF16), wb_ref[1])
              + _sigmoid(proj(G_GATES, D_MODEL, 2 * D_MODEL)) * _dot(h_hg.astype(BF16), wb_ref[2]))
    y = _dot(merged.astype(BF16), wo_ref[...])
    r = x + y * lax.rsqrt(jnp.mean(y * y, axis=-1, keepdims=True) + RMS_EPS) * gpost_ref[...]
    gate = _sigmoid(_dot(r.astype(BF16), wpg_ref[...]))
    r_ref[...] = r + gate * _dot(p_ref[...].astype(BF16), wpp_ref[...])


def _merge(l, hm, o_s, o_h, x2d, p_all, lw):
    n = x2d.shape[0]
    tm = min(n, MERGE_ROWS)
    rows = lambda width: pl.BlockSpec((tm, width), lambda i: (i, 0))
    full = lambda a: pl.BlockSpec(a.shape, lambda i: (0,) * a.ndim, pipeline_mode=pl.Buffered(1))
    params = [lw["g_pre"], *lw["w_gate"]] + [lw[k] for k in ("g_mlh", "g_hgh", "w_branch", "w_out", "g_post", "w_pg",
                                                              "w_pp")]
    return pl.pallas_call(
        _merge_kernel,
        grid=(n // tm,),
        in_specs=[rows(W), rows(W), rows(W), rows(D_MODEL), pl.BlockSpec((None, tm, PLE_DIM), lambda i: (l, i, 0))]
        + [full(a) for a in params],
        out_specs=rows(D_MODEL),
        out_shape=jax.ShapeDtypeStruct((n, D_MODEL), F32),
        compiler_params=pltpu.CompilerParams(
            dimension_semantics=("parallel",), vmem_limit_bytes=VMEM_LIMIT),
        name="merge",
    )(hm, o_s, o_h, x2d, p_all, *params)


def _layer(l, depth, x, p_all, c0, n0, m0, conv0, s0, k_past, v_past, kv_prev, lw):
    B, T, _ = x.shape
    proj2d, gates2d, sbq2d, k_rows, v_rows = _inproj(l, depth, x, lw["g_pre"], lw["w_in"], lw["w_if"], kv_prev)
    proj = proj2d.reshape(B, T, N_PROJ)

    conv0_pad = jnp.pad(conv0, ((0, 0), (HALO - (CONV_WIDTH - 1), 0), (0, 0)))
    hm, c_new, n_new, m_new, o_h, st_new = _recurrent(
        proj, gates2d.reshape(B, T, 2 * LANE), conv0_pad, lw["w_conv"], lw["bias_row"], c0,
        n0.reshape(B, N_HEADS, 1, HEAD_DIM),
        jnp.broadcast_to(m0[:, :, None, None], (B, N_HEADS, 1, LANE)),
        lw["log_lb"], lw["log1m_lb"], lw["one_m_lb"], jnp.swapaxes(s0, -1, -2))
    o_s = _sb_attention(l, sbq2d.reshape(B, T, BRANCH_WIDTH), k_rows, v_rows, k_past, v_past)

    flat = lambda a: a.reshape(B * T, a.shape[-1])
    r = _merge(l, flat(hm), flat(o_s), flat(o_h), flat(x), p_all.reshape(depth, B * T, PLE_DIM), lw)

    conv_new = proj[:, T - (CONV_WIDTH - 1):, C_MLQ * LANE:C_MLQ * LANE + 2 * BRANCH_WIDTH]
    states = (c_new, n_new[:, :, 0, :], m_new[:, :, 0, 0], conv_new, jnp.swapaxes(st_new, -1, -2))
    return r.reshape(B, T, D_MODEL), states, (k_rows, v_rows)


def _layer_weights(l, lower_bounds, g_pre, w_in, b_i, b_f, w_conv, g_mlh, g_hgh, w_branch, w_out, g_post, w_pg, w_pp):
    cols = lambda span: w_in[l, :, span[0]:span[1]].astype(BF16)
    lane_pad = lambda a: jnp.pad(a, ((0, 0), (0, LANE - a.shape[1])))
    w_if = jnp.concatenate([lane_pad(cols(SRC_ML_I)), lane_pad(cols(SRC_ML_F))], axis=1)
    w_re = jnp.stack([cols(span) for span in INPROJ_SOURCES])
    w_gate = [cols(span) for span in MERGE_SOURCES]
    bias_row = jnp.concatenate([jnp.pad(b_i[l], (0, LANE - N_HEADS)), jnp.pad(b_f[l], (0, LANE - N_HEADS))])
    bias_row = bias_row.reshape(1, 2 * LANE)
    lb = lower_bounds[l].reshape(1, BRANCH_WIDTH)
    return dict(
        g_pre=g_pre[l].reshape(1, D_MODEL), w_in=w_re, w_if=w_if, w_gate=w_gate, bias_row=bias_row,
        w_conv=w_conv[l],
        g_mlh=g_mlh[l].reshape(1, BRANCH_WIDTH), g_hgh=g_hgh[l].reshape(1, BRANCH_WIDTH),
        log_lb=jnp.log(lb) * LOG2_E, log1m_lb=jnp.log1p(-lb) * LOG2_E, one_m_lb=1.0 - lb,
        w_branch=w_branch[l].astype(BF16), w_out=w_out[l].astype(BF16), g_post=g_post[l].reshape(1, D_MODEL),
        w_pg=w_pg[l].astype(BF16), w_pp=w_pp[l].astype(BF16))


def kernel(x_prompt, x_sample, p_prompt, p_sample, state_mlstm_C, state_mlstm_n, state_mlstm_m, state_mlstm_conv,
           cache_sb_k, cache_sb_v, state_hgrn_S, g_pre, w_in, b_mlstm_i, b_mlstm_f, w_mlstm_conv, g_mlstm_head,
           hgrn_lb_logits, g_hgrn_head, w_branch, w_out, g_post, w_ple_gate, w_ple_proj):
    depth = w_in.shape[0]
    lb_cum = jnp.cumsum(jax.nn.softmax(hgrn_lb_logits.astype(F32), axis=0), axis=0)
    lower_bounds = lb_cum - lb_cum[:1]
    B = x_prompt.shape[0]
    yp, ys = x_prompt, x_sample
    new_p, new_s = [], []
    kv_p = kv_s = None
    for l in range(depth):
        lw = _layer_weights(l, lower_bounds, g_pre, w_in, b_mlstm_i, b_mlstm_f, w_mlstm_conv, g_mlstm_head,
                            g_hgrn_head, w_branch, w_out, g_post, w_ple_gate, w_ple_proj)
        zeros = lambda *shape: jnp.zeros(shape, F32)
        yp, st, kv_p = _layer(l, depth, yp, p_prompt, zeros(B, N_HEADS, HEAD_DIM, HEAD_DIM),
                              zeros(B, N_HEADS, HEAD_DIM), zeros(B, N_HEADS),
                              zeros(B, CONV_WIDTH - 1, 2 * BRANCH_WIDTH), zeros(B, N_HEADS, HEAD_DIM, HEAD_DIM),
                              None, None, kv_p, lw)
        new_p.append(st)
        ys, st, kv_s = _layer(l, depth, ys, p_sample, state_mlstm_C[l], state_mlstm_n[l], state_mlstm_m[l],
                              state_mlstm_conv[l], state_hgrn_S[l], cache_sb_k, cache_sb_v, kv_s, lw)
        new_s.append(st)
    stack = lambda sts, i: jnp.stack([s[i] for s in sts])
    outs = lambda sts, kv: (stack(sts, 0), stack(sts, 1), stack(sts, 2), stack(sts, 3), kv[0], kv[1], stack(sts, 4))
    return (yp, ys, *outs(new_p, kv_p), *outs(new_s, kv_s))
```
